```python
import functools
import jax, jax.numpy as jnp
from jax import lax
import numpy as np

D_MODEL = 2048
BATCH = 4
SEQ = 4096
DEPTH = 1
DEC_BATCH = 32
DEC_SEQ = 4
PAST_LEN = 16384
PAGE_SIZE = 128

HEAD_DIM = 64
SB_HEADS = 16
RW_HEADS = 16
SB_WIDTH = SB_HEADS * HEAD_DIM
RW_WIDTH = RW_HEADS * HEAD_DIM
MIX_WIDTH = SB_WIDTH + RW_WIDTH
DECAY_LORA = 64
ICL_LORA = 64
GATE_LORA = 160
RW_SHIFT_W = 3 * RW_WIDTH + DECAY_LORA + ICL_LORA + GATE_LORA
IN_WIDTH = 3 * SB_WIDTH + RW_SHIFT_W
D_FF = 5632
CONV_W = 3
Q_BLOCK = 128
NORM_EPS = 1e-6
LNX_EPS = 64e-5
SB_BIAS_INIT = -8.0

kernel_name = 'stickbreak_rwkv7_hymba_step'


def rmsnorm(x, g):
    xf = x.astype(jnp.float32)
    y = xf * lax.rsqrt(jnp.mean(xf * xf, axis=-1, keepdims=True) + NORM_EPS)
    return (y * g.astype(jnp.float32)).astype(x.dtype)


def sb_block(q, k, v, bias, mask, carry):
    z = (jnp.einsum('bqhd,bkhd->bhqk', q, k).astype(jnp.float32) * (HEAD_DIM ** -0.5)
         + bias.astype(jnp.float32)[None, :, None, None])
    log_pass = jnp.where(mask, jax.nn.log_sigmoid(-z), 0.0)
    after = lax.cumsum(log_pass, axis=3, reverse=True) - log_pass
    wts = jnp.where(mask, jnp.exp(jax.nn.log_sigmoid(z) + after + carry[..., None]), 0.0)
    out = jnp.einsum('bhqk,bkhd->bqhd', wts, v.astype(jnp.float32))
    return out, carry + jnp.sum(log_pass, axis=3)


def sb_prompt(q, k, v, bias):
    B, S = q.shape[0], q.shape[1]
    nb = S // Q_BLOCK
    qb = jnp.swapaxes(q.reshape(B, nb, Q_BLOCK, SB_HEADS, HEAD_DIM), 0, 1)
    kpos = jnp.arange(S)

    def one_block(args):
        i, qi = args
        qpos = i * Q_BLOCK + jnp.arange(Q_BLOCK)
        mask = kpos[None, :] < qpos[:, None]
        out, _ = sb_block(qi, k, v, bias, mask, jnp.zeros((B, SB_HEADS, Q_BLOCK), jnp.float32))
        return out

    outs = lax.map(one_block, (jnp.arange(nb), qb))
    return jnp.swapaxes(outs, 0, 1).reshape(B, S, SB_HEADS, HEAD_DIM).astype(q.dtype)


def sb_sample(q, k, v, bias, cache_k, cache_v, page_table):
    B, T = q.shape[0], q.shape[1]
    causal = jnp.arange(T)[None, :] < jnp.arange(T)[:, None]
    out, carry = sb_block(q, k, v, bias, causal, jnp.zeros((B, SB_HEADS, T), jnp.float32))
    full = jnp.ones((T, PAGE_SIZE), dtype=bool)

    def page_step(c, pages):
        acc, cr = c
        o, cr = sb_block(q, cache_k[pages], cache_v[pages], bias, full, cr)
        return (acc + o, cr), None

    (out, _), _ = lax.scan(page_step, (out, carry), page_table.T[::-1])
    return out.astype(q.dtype)


def rwkv_mix(u, shift_prev, wkv0, p):
    B, T = u.shape[0], u.shape[1]
    u_prev = jnp.concatenate([shift_prev[:, None, :].astype(u.dtype), u[:, :-1]], axis=1)
    us = u + (u_prev - u) * p['rw_mu']
    o1, o2, o3 = RW_WIDTH, 2 * RW_WIDTH, 3 * RW_WIDTH
    o4, o5 = o3 + DECAY_LORA, o3 + DECAY_LORA + ICL_LORA
    r, k, v = us[..., :o1], us[..., o1:o2], us[..., o2:o3]
    w_lo, a_lo, g_lo = us[..., o3:o4], us[..., o4:o5], us[..., o5:]
    w_log = -jax.nn.softplus(-(p['rw_w0'] + jnp.tanh(w_lo) @ p['rw_w2'])) - 0.5
    decay = jnp.exp(-jnp.exp(w_log.astype(jnp.float32)))
    a = jax.nn.sigmoid(p['rw_a0'] + a_lo @ p['rw_a2'])
    gate = jax.nn.sigmoid(g_lo) @ p['rw_g2']
    heads = lambda t: t.astype(jnp.float32).reshape(B, T, RW_HEADS, HEAD_DIM)
    kk = heads(k * p['rw_kk'])
    kk = kk / jnp.maximum(jnp.sqrt(jnp.sum(kk * kk, axis=-1, keepdims=True)), 1e-12)
    k_eff = k * (1.0 + (a - 1.0) * p['rw_ka'])
    r_h, k_h, v_h, a_h, w_h = heads(r), heads(k_eff), heads(v), heads(a), heads(decay)

    def step(S, inp):
        r_t, k_t, v_t, w_t, kk_t, a_t = inp
        sa = jnp.einsum('bhij,bhj->bhi', S, -kk_t)
        S = (S * w_t[:, :, None, :] + sa[..., None] * (kk_t * a_t)[:, :, None, :]
             + v_t[..., None] * k_t[:, :, None, :])
        return S, jnp.einsum('bhij,bhj->bhi', S, r_t)

    seq_first = lambda t: jnp.swapaxes(t, 0, 1)
    S, y = lax.scan(step, wkv0.astype(jnp.float32),
                    tuple(seq_first(t) for t in (r_h, k_h, v_h, w_h, kk, a_h)))
    y = seq_first(y)
    mu = jnp.mean(y, axis=-1, keepdims=True)
    var = jnp.mean(jnp.square(y - mu), axis=-1, keepdims=True)
    yn = ((y - mu) * lax.rsqrt(var + LNX_EPS)).reshape(B, T, RW_WIDTH) * p['rw_ln_g'] + p['rw_ln_b']
    bonus = jnp.sum(r_h * k_h * p['rw_rk'], axis=-1, keepdims=True) * v_h
    out = (yn + bonus.reshape(B, T, RW_WIDTH)) * gate
    return out.astype(u.dtype), S.astype(wkv0.dtype), u[:, -1]


def conv_ffn(h, conv_prev, p):
    T = h.shape[1]
    up = h @ p['w_up']
    gate_in, val = up[..., :D_FF], up[..., D_FF:]
    ext = jnp.concatenate([conv_prev.astype(h.dtype), gate_in], axis=1)
    conv = p['conv_b'] + p['conv_w'][0] * ext[:, 0:T]
    for i in range(1, CONV_W):
        conv = conv + p['conv_w'][i] * ext[:, i:i + T]
    y = (jax.nn.gelu(conv) * val) @ p['w_down']
    return y, ext[:, T:]


def mixer_layer(x, shift_prev, wkv0, conv_prev, sb_attend, p):
    B, T = x.shape[0], x.shape[1]
    h = rmsnorm(x, p['g_pre_mix'])
    proj = h @ p['w_in']
    q = proj[..., :SB_WIDTH].reshape(B, T, SB_HEADS, HEAD_DIM)
    k = proj[..., SB_WIDTH:2 * SB_WIDTH].reshape(B, T, SB_HEADS, HEAD_DIM)
    v = proj[..., 2 * SB_WIDTH:3 * SB_WIDTH].reshape(B, T, SB_HEADS, HEAD_DIM)
    u = proj[..., 3 * SB_WIDTH:]
    o_sb = rmsnorm(sb_attend(q, k, v, p['sb_bias']),
                   p['g_sb_out'].reshape(SB_HEADS, HEAD_DIM)).reshape(B, T, SB_WIDTH)
    o_rw, wkv_new, shift_new = rwkv_mix(u, shift_prev, wkv0, p)
    mixed = jnp.concatenate([o_sb, o_rw], axis=-1) @ p['w_out']
    x = x + rmsnorm(mixed, p['g_post_mix'])
    f, conv_new = conv_ffn(rmsnorm(x, p['g_pre_ffn']), conv_prev, p)
    x = x + rmsnorm(f, p['g_post_ffn'])
    return x, k, v, wkv_new, shift_new, conv_new


def setup_inputs(seed: int = 0) -> dict:
    key = jax.random.key(seed)
    ks = jax.random.split(key, 32)
    n_pages = PAST_LEN // PAGE_SIZE
    n_pool = (DEC_BATCH * n_pages * 5) // 4
    nrm = lambda kk, shape, scale: jax.random.normal(kk, shape, jnp.float32) * scale
    page_table = jax.random.permutation(ks[7], n_pool)[:DEC_BATCH * n_pages]
    page_table = page_table.reshape(DEC_BATCH, n_pages).astype(jnp.int32)
    return {
        'x_prompt': nrm(ks[0], (BATCH, SEQ, D_MODEL), 1.0),
        'x_sample': nrm(ks[1], (DEC_BATCH, DEC_SEQ, D_MODEL), 1.0),
        'cache_sb_k': nrm(ks[2], (DEPTH, n_pool, PAGE_SIZE, SB_HEADS, HEAD_DIM), 1.0),
        'cache_sb_v': nrm(ks[3], (DEPTH, n_pool, PAGE_SIZE, SB_HEADS, HEAD_DIM), 1.0),
        'state_rwkv_wkv': nrm(ks[4], (DEPTH, DEC_BATCH, RW_HEADS, HEAD_DIM, HEAD_DIM), 0.1),
        'state_rwkv_shift': nrm(ks[5], (DEPTH, DEC_BATCH, RW_SHIFT_W), 1.0),
        'state_ffn_conv': nrm(ks[6], (DEPTH, DEC_BATCH, CONV_W - 1, D_FF), 1.0),
        'page_table': page_table,
        'w_in': nrm(ks[8], (DEPTH, D_MODEL, IN_WIDTH), D_MODEL ** -0.5),
        'w_out': nrm(ks[9], (DEPTH, MIX_WIDTH, D_MODEL), MIX_WIDTH ** -0.5),
        'g_pre_mix': 1.0 + nrm(ks[10], (DEPTH, D_MODEL), 0.1),
        'g_post_mix': 1.0 + nrm(ks[11], (DEPTH, D_MODEL), 0.1),
        'g_pre_ffn': 1.0 + nrm(ks[12], (DEPTH, D_MODEL), 0.1),
        'g_post_ffn': 1.0 + nrm(ks[13], (DEPTH, D_MODEL), 0.1),
        'g_sb_out': 1.0 + nrm(ks[14], (DEPTH, SB_WIDTH), 0.1),
        'sb_bias': SB_BIAS_INIT + nrm(ks[30], (DEPTH, SB_HEADS), 0.5),
        'rw_mu': jax.random.uniform(ks[15], (DEPTH, RW_SHIFT_W), jnp.float32, 0.0, 1.0),
        'rw_w0': jax.random.uniform(ks[16], (DEPTH, RW_WIDTH), jnp.float32, -6.0, -1.0),
        'rw_w2': nrm(ks[17], (DEPTH, DECAY_LORA, RW_WIDTH), 0.5 * DECAY_LORA ** -0.5),
        'rw_a0': nrm(ks[18], (DEPTH, RW_WIDTH), 0.1),
        'rw_a2': nrm(ks[19], (DEPTH, ICL_LORA, RW_WIDTH), 0.5 * ICL_LORA ** -0.5),
        'rw_g2': nrm(ks[20], (DEPTH, GATE_LORA, RW_WIDTH), GATE_LORA ** -0.5),
        'rw_kk': 1.0 + nrm(ks[21], (DEPTH, RW_WIDTH), 0.1),
        'rw_ka': 1.0 + nrm(ks[22], (DEPTH, RW_WIDTH), 0.1),
        'rw_rk': nrm(ks[23], (DEPTH, RW_HEADS, HEAD_DIM), 0.1),
        'rw_ln_g': 1.0 + nrm(ks[24], (DEPTH, RW_WIDTH), 0.1),
        'rw_ln_b': nrm(ks[25], (DEPTH, RW_WIDTH), 0.01),
        'w_up': nrm(ks[26], (DEPTH, D_MODEL, 2 * D_FF), D_MODEL ** -0.5),
        'conv_w': nrm(ks[27], (DEPTH, CONV_W, D_FF), CONV_W ** -0.5),
        'conv_b': nrm(ks[28], (DEPTH, D_FF), 0.01),
        'w_down': nrm(ks[29], (DEPTH, D_FF, D_MODEL), D_FF ** -0.5),
    }


def reference(x_prompt, x_sample, cache_sb_k, cache_sb_v, state_rwkv_wkv, state_rwkv_shift,
              state_ffn_conv, page_table, w_in, w_out, g_pre_mix, g_post_mix, g_pre_ffn,
              g_post_ffn, g_sb_out, sb_bias, rw_mu, rw_w0, rw_w2, rw_a0, rw_a2, rw_g2, rw_kk,
              rw_ka, rw_rk, rw_ln_g, rw_ln_b, w_up, conv_w, conv_b, w_down):
    yp, ys = x_prompt, x_sample
    bp = x_prompt.shape[0]
    kp_l, vp_l, sp_l, hp_l, cp_l = [], [], [], [], []
    ks_l, vs_l, ss_l, hs_l, cs_l = [], [], [], [], []
    for l in range(DEPTH):
        p = {'w_in': w_in[l], 'w_out': w_out[l], 'g_pre_mix': g_pre_mix[l],
             'g_post_mix': g_post_mix[l], 'g_pre_ffn': g_pre_ffn[l], 'g_post_ffn': g_post_ffn[l],
             'g_sb_out': g_sb_out[l], 'sb_bias': sb_bias[l], 'rw_mu': rw_mu[l], 'rw_w0': rw_w0[l],
             'rw_w2': rw_w2[l], 'rw_a0': rw_a0[l], 'rw_a2': rw_a2[l], 'rw_g2': rw_g2[l],
             'rw_kk': rw_kk[l], 'rw_ka': rw_ka[l], 'rw_rk': rw_rk[l], 'rw_ln_g': rw_ln_g[l],
             'rw_ln_b': rw_ln_b[l], 'w_up': w_up[l], 'conv_w': conv_w[l], 'conv_b': conv_b[l],
             'w_down': w_down[l]}
        yp, kp, vp, sp, hp, cp = mixer_layer(
            yp, jnp.zeros((bp, RW_SHIFT_W), yp.dtype),
            jnp.zeros((bp, RW_HEADS, HEAD_DIM, HEAD_DIM), yp.dtype),
            jnp.zeros((bp, CONV_W - 1, D_FF), yp.dtype), sb_prompt, p)
        attend = functools.partial(sb_sample, cache_k=cache_sb_k[l], cache_v=cache_sb_v[l],
                                   page_table=page_table)
        ys, k_s, v_s, s_s, h_s, c_s = mixer_layer(
            ys, state_rwkv_shift[l], state_rwkv_wkv[l], state_ffn_conv[l], attend, p)
        kp_l.append(kp); vp_l.append(vp); sp_l.append(sp); hp_l.append(hp); cp_l.append(cp)
        ks_l.append(k_s); vs_l.append(v_s); ss_l.append(s_s); hs_l.append(h_s); cs_l.append(c_s)
    return (yp, ys,
            jnp.stack(kp_l), jnp.stack(vp_l), jnp.stack(sp_l), jnp.stack(hp_l), jnp.stack(cp_l),
            jnp.stack(ks_l), jnp.stack(vs_l), jnp.stack(ss_l), jnp.stack(hs_l), jnp.stack(cs_l))
```

```python
import functools
import math

import jax
import jax.numpy as jnp
from jax import lax
from jax.experimental import pallas as pl
from jax.experimental.pallas import tpu as pltpu

F32 = jnp.float32
BF16 = jnp.bfloat16

HEAD_DIM = 64
LANES = 128
HEADS_PER_TILE = LANES // HEAD_DIM
NORM_EPS = 1e-6
LNX_EPS = 64e-5
KK_EPS = 1e-12
CONV_W = 3
PAGE_SIZE = 128
VMEM_LIMIT = 48 * 1024 * 1024
RW_CHUNK = 64


def _dot(a, b):
    return jnp.dot(a, b, preferred_element_type=F32)


def _dot_nt(a, b):
    return lax.dot_general(a, b, (((1,), (1,)), ((), ())), preferred_element_type=F32)


def _dot_tn(a, b):
    return lax.dot_general(a, b, (((0,), (0,)), ((), ())), preferred_element_type=F32)


def _split2(x):
    hi = x.astype(BF16)
    lo = (x - hi.astype(F32)).astype(BF16)
    return hi, lo


def _split3(x):
    hi = x.astype(BF16)
    r1 = x - hi.astype(F32)
    mid = r1.astype(BF16)
    lo = (r1 - mid.astype(F32)).astype(BF16)
    return hi, mid, lo


def _dot_exact_rhs(x, m01):
    hi, lo = _split2(x)
    return _dot(hi, m01) + _dot(lo, m01)


def _dot3(a, b_hi, b_lo):
    a_hi, a_lo = _split2(a)
    return _dot(a_hi, b_hi) + (_dot(a_lo, b_hi) + _dot(a_hi, b_lo))


def _softplus(x):
    return jnp.maximum(x, 0.0) + jnp.log(1.0 + jnp.exp(-jnp.abs(x)))


def _sigmoid(x):
    return 1.0 / (1.0 + jnp.exp(-x))


def _rms_rows(x, g):
    return x * lax.rsqrt(jnp.mean(x * x, axis=-1, keepdims=True) + NORM_EPS) * g


def _params(*sem):
    return pltpu.CompilerParams(dimension_semantics=sem, vmem_limit_bytes=VMEM_LIMIT)


def _head_block_ones(n):
    i = jnp.arange(n) // HEAD_DIM
    return (i[:, None] == i[None, :]).astype(BF16)


def _in_proj_kernel(x_ref, g_ref, w_ref, q_ref, k_ref, v_ref, u_ref, h_ref, *, nq, nk, nv):
    j = pl.program_id(1)

    @pl.when(j == 0)
    def _():
        h_ref[...] = _rms_rows(x_ref[...], g_ref[...]).astype(BF16)

    res = _dot(h_ref[...], w_ref[...])

    @pl.when(j < nq)
    def _():
        q_ref[...] = (res * (HEAD_DIM ** -0.5)).astype(BF16)

    @pl.when((j >= nq) & (j < nq + nk))
    def _():
        k_ref[...] = res

    @pl.when((j >= nq + nk) & (j < nq + nk + nv))
    def _():
        v_ref[...] = res

    @pl.when(j >= nq + nk + nv)
    def _():
        u_ref[...] = res


def in_proj(x, g, w, sb_width, u_width, *, tm, tn=512):
    n, d = x.shape
    tm = min(tm, n)
    nq = nk = nv = sb_width // tn
    nu = u_width // tn
    nj = nq + nk + nv + nu
    off_k, off_v, off_u = nq, nq + nk, nq + nk + nv
    clip = lambda j, lo, cnt: jnp.clip(j - lo, 0, cnt - 1)
    kern = functools.partial(_in_proj_kernel, nq=nq, nk=nk, nv=nv)
    return pl.pallas_call(
        kern,
        grid=(n // tm, nj),
        in_specs=[
            pl.BlockSpec((tm, d), lambda i, j: (i, 0)),
            pl.BlockSpec((1, d), lambda i, j: (0, 0)),
            pl.BlockSpec((d, tn), lambda i, j: (0, j)),
        ],
        out_specs=[
            pl.BlockSpec((tm, tn), lambda i, j: (i, clip(j, 0, nq))),
            pl.BlockSpec((tm, tn), lambda i, j: (i, clip(j, off_k, nk))),
            pl.BlockSpec((tm, tn), lambda i, j: (i, clip(j, off_v, nv))),
            pl.BlockSpec((tm, tn), lambda i, j: (i, clip(j, off_u, nu))),
        ],
        out_shape=[
            jax.ShapeDtypeStruct((n, sb_width), BF16),
            jax.ShapeDtypeStruct((n, sb_width), F32),
            jax.ShapeDtypeStruct((n, sb_width), F32),
            jax.ShapeDtypeStruct((n, u_width), F32),
        ],
        scratch_shapes=[pltpu.VMEM((tm, d), BF16)],
        compiler_params=_params("parallel", "arbitrary"),
        name="in_proj",
    )(x, g, w)


def _mm_norm_resid_kernel(a_ref, w_ref, r_ref, g_ref, g2_ref, *rest, nk, emit_h):
    if emit_h:
        o_ref, h_ref, acc_ref = rest
    else:
        o_ref, acc_ref = rest
        h_ref = None
    kk = pl.program_id(1)
    part = _dot(a_ref[...], w_ref[...])

    def finalize(m):
        y = r_ref[...] + _rms_rows(m, g_ref[...])
        o_ref[...] = y
        if emit_h:
            h_ref[...] = _rms_rows(y, g2_ref[...]).astype(BF16)

    if nk == 1:
        finalize(part)
    else:
        @pl.when(kk == 0)
        def _():
            acc_ref[...] = part

        @pl.when((kk > 0) & (kk < nk - 1))
        def _():
            acc_ref[...] += part

        @pl.when(kk == nk - 1)
        def _():
            finalize(acc_ref[...] + part)


def mm_norm_resid(a, w, resid, g, g2, *, tm, tk, emit_h):
    n, kdim = a.shape
    d = w.shape[1]
    tm = min(tm, n)
    nk = kdim // tk
    kern = functools.partial(_mm_norm_resid_kernel, nk=nk, emit_h=emit_h)
    out_specs = [pl.BlockSpec((tm, d), lambda i, k: (i, 0))]
    out_shape = [jax.ShapeDtypeStruct((n, d), F32)]
    if emit_h:
        out_specs.append(pl.BlockSpec((tm, d), lambda i, k: (i, 0)))
        out_shape.append(jax.ShapeDtypeStruct((n, d), BF16))
    return pl.pallas_call(
        kern,
        grid=(n // tm, nk),
        in_specs=[
            pl.BlockSpec((tm, tk), lambda i, k: (i, k)),
            pl.BlockSpec((tk, d), lambda i, k: (k, 0)),
            pl.BlockSpec((tm, d), lambda i, k: (i, 0)),
            pl.BlockSpec((1, d), lambda i, k: (0, 0)),
            pl.BlockSpec((1, d), lambda i, k: (0, 0)),
        ],
        out_specs=out_specs,
        out_shape=out_shape,
        scratch_shapes=[pltpu.VMEM((tm, d), F32)],
        compiler_params=_params("parallel", "arbitrary"),
        name="mm_norm_resid",
    )(a, w, resid, g, g2)


def _gelu_tanh(x):
    c = math.sqrt(2.0 / math.pi)
    return x * (0.5 * (1.0 + jnp.tanh(c * (x + 0.044715 * (x * x * x)))))


def _ffn_up_kernel(*refs, tiles_per_seq, seq_len, halo_mode):
    if halo_mode:
        h_ref, halo_ref, wg_ref, wv_ref, cw_ref, cb_ref, prev_ref, act_ref, g_out_ref = refs
    else:
        h_ref, wg_ref, wv_ref, cw_ref, cb_ref, p1_ref, p2_ref, act_ref, g_out_ref = refs
    h = h_ref[...]
    g = _dot(h, wg_ref[...])
    val = _dot(h, wv_ref[...])
    tm = g.shape[0]
    row = lax.broadcasted_iota(jnp.int32, (tm, 1), 0)
    r1 = pltpu.roll(g, 1, axis=0)
    r2 = pltpu.roll(g, 2, axis=0)
    if halo_mode:
        gh = _dot(halo_ref[...], wg_ref[...])
        first = (pl.program_id(0) % tiles_per_seq) == 0
        pm1 = jnp.where(first, prev_ref[0, 1:2, :], gh[7:8, :])
        pm2 = jnp.where(first, prev_ref[0, 0:1, :], gh[6:7, :])
        g1 = jnp.where(row == 0, pm1, r1)
        g2 = jnp.where(row == 0, pm2, jnp.where(row == 1, pm1, r2))
        g_out_ref[0] = g[tm - 8:, :]
    else:
        t = row % seq_len
        g1 = jnp.where(t >= 1, r1, p1_ref[...])
        g2 = jnp.where(t >= 2, r2, p2_ref[...])
        g_out_ref[...] = g
    conv = cb_ref[...] + cw_ref[0:1, :] * g2 + cw_ref[1:2, :] * g1 + cw_ref[2:3, :] * g
    act_ref[...] = (_gelu_tanh(conv) * val).astype(BF16)


def ffn_up_prompt(h, w_up, conv_w, conv_b, conv_prev, seq_len, *, tm, tn=512):
    n, d = h.shape
    f = w_up.shape[1] // 2
    nf = f // tn
    tps = seq_len // tm
    kern = functools.partial(_ffn_up_kernel, tiles_per_seq=tps, seq_len=seq_len, halo_mode=True)
    halo_blk = tm // 8
    return pl.pallas_call(
        kern,
        grid=(n // tm, nf),
        in_specs=[
            pl.BlockSpec((tm, d), lambda i, j: (i, 0)),
            pl.BlockSpec((8, d), lambda i, j: (jnp.maximum(i * halo_blk - 1, 0), 0)),
            pl.BlockSpec((d, tn), lambda i, j: (0, j)),
            pl.BlockSpec((d, tn), lambda i, j: (0, j + nf)),
            pl.BlockSpec((CONV_W, tn), lambda i, j: (0, j)),
            pl.BlockSpec((1, tn), lambda i, j: (0, j)),
            pl.BlockSpec((1, CONV_W - 1, tn), lambda i, j: (i // tps, 0, j)),
        ],
        out_specs=[
            pl.BlockSpec((tm, tn), lambda i, j: (i, j)),
            pl.BlockSpec((1, 8, tn), lambda i, j: (i, 0, j)),
        ],
        out_shape=[
            jax.ShapeDtypeStruct((n, f), BF16),
            jax.ShapeDtypeStruct((n // tm, 8, f), F32),
        ],
        compiler_params=_params("parallel", "arbitrary"),
        name="ffn_up_prompt",
    )(h, h, w_up, w_up, conv_w, conv_b, conv_prev)


def ffn_up_sample(h, w_up, conv_w, conv_b, p1, p2, seq_len, *, tn=512):
    n, d = h.shape
    f = w_up.shape[1] // 2
    nf = f // tn
    kern = functools.partial(_ffn_up_kernel, tiles_per_seq=1, seq_len=seq_len, halo_mode=False)
    return pl.pallas_call(
        kern,
        grid=(1, nf),
        in_specs=[
            pl.BlockSpec((n, d), lambda i, j: (0, 0)),
            pl.BlockSpec((d, tn), lambda i, j: (0, j)),
            pl.BlockSpec((d, tn), lambda i, j: (0, j + nf)),
            pl.BlockSpec((CONV_W, tn), lambda i, j: (0, j)),
            pl.BlockSpec((1, tn), lambda i, j: (0, j)),
            pl.BlockSpec((n, tn), lambda i, j: (0, j)),
            pl.BlockSpec((n, tn), lambda i, j: (0, j)),
        ],
        out_specs=[
            pl.BlockSpec((n, tn), lambda i, j: (0, j)),
            pl.BlockSpec((n, tn), lambda i, j: (0, j)),
        ],
        out_shape=[
            jax.ShapeDtypeStruct((n, f), BF16),
            jax.ShapeDtypeStruct((n, f), F32),
        ],
        compiler_params=_params("arbitrary", "arbitrary"),
        name="ffn_up_sample",
    )(h, w_up, w_up, conv_w, conv_b, p1, p2)


def _sb_tile(qm, kblk, vblk, bias, later_ref, ones_ref, carry_ref, acc_ref, mask):
    z = _dot_nt(qm, kblk) + bias
    sp = _softplus(z)
    if mask is not None:
        sp = jnp.where(mask, sp, 0.0)
    spb = sp.astype(BF16)
    after = _dot(spb, later_ref[...])
    arg = z - sp - after - carry_ref[...]
    w = jnp.exp(arg)
    if mask is not None:
        w = jnp.where(mask, w, 0.0)
    acc_ref[...] += _dot(w.astype(BF16), vblk)
    carry_ref[...] += _dot(spb, ones_ref[...])


def _sb_prompt_kernel(bias_ref, q_ref, k_ref, v_ref, later_ref, ones_ref, bd_ref, g_ref, o_ref,
                      kb_ref, vb_ref, carry_ref, acc_ref, *, tq):
    p = pl.program_id(1)
    qi = pl.program_id(2)

    @pl.when(qi == 0)
    def _():
        kb_ref[...] = k_ref[...].astype(BF16)
        vb_ref[...] = v_ref[...].astype(BF16)

    q = q_ref[...]
    lane = lax.broadcasted_iota(jnp.int32, (1, LANES), 1)
    row = lax.broadcasted_iota(jnp.int32, (tq, tq), 0)
    col = lax.broadcasted_iota(jnp.int32, (tq, tq), 1)
    causal = col < row
    outs = []
    for hh in range(HEADS_PER_TILE):
        in_head = (lane // HEAD_DIM) == hh
        qm = jnp.where(in_head, q, jnp.zeros_like(q))
        bias = bias_ref[p * HEADS_PER_TILE + hh]
        carry_ref[...] = jnp.zeros_like(carry_ref)
        acc_ref[...] = jnp.zeros_like(acc_ref)
        d0 = pl.multiple_of(qi * tq, tq)
        _sb_tile(qm, kb_ref[pl.ds(d0, tq), :], vb_ref[pl.ds(d0, tq), :], bias,
                 later_ref, ones_ref, carry_ref, acc_ref, causal)

        def body(it, c):
            j0 = pl.multiple_of((qi - 1 - it) * tq, tq)
            _sb_tile(qm, kb_ref[pl.ds(j0, tq), :], vb_ref[pl.ds(j0, tq), :], bias,
                     later_ref, ones_ref, carry_ref, acc_ref, None)
            return c

        lax.fori_loop(0, qi, body, 0)
        outs.append(acc_ref[...])
    o = jnp.where((lane // HEAD_DIM) == 0, outs[0], outs[1])
    ms = _dot_exact_rhs(o * o, bd_ref[...]) * (1.0 / HEAD_DIM)
    o_ref[...] = (o * lax.rsqrt(ms + NORM_EPS) * g_ref[...]).astype(BF16)


def sb_prompt(q, k, v, bias, g_sb, batch, seq_len, *, tq):
    n, w = q.shape
    npairs = w // LANES
    nq = seq_len // tq
    idx = jnp.arange(tq)
    later = (idx[:, None] > idx[None, :]).astype(BF16)
    ones = jnp.ones((tq, tq), BF16)
    bd = _head_block_ones(LANES)
    kern = functools.partial(_sb_prompt_kernel, tq=tq)
    const = lambda b, p, i: (0, 0)
    return pl.pallas_call(
        kern,
        grid=(batch, npairs, nq),
        in_specs=[
            pl.BlockSpec(memory_space=pltpu.SMEM),
            pl.BlockSpec((tq, LANES), lambda b, p, i: (b * nq + i, p)),
            pl.BlockSpec((seq_len, LANES), lambda b, p, i: (b, p)),
            pl.BlockSpec((seq_len, LANES), lambda b, p, i: (b, p)),
            pl.BlockSpec((tq, tq), const),
            pl.BlockSpec((tq, tq), const),
            pl.BlockSpec((LANES, LANES), const),
            pl.BlockSpec((1, LANES), lambda b, p, i: (0, p)),
        ],
        out_specs=pl.BlockSpec((tq, LANES), lambda b, p, i: (b * nq + i, p)),
        out_shape=jax.ShapeDtypeStruct((n, w), BF16),
        scratch_shapes=[
            pltpu.VMEM((seq_len, LANES), BF16),
            pltpu.VMEM((seq_len, LANES), BF16),
            pltpu.VMEM((tq, tq), F32),
            pltpu.VMEM((tq, LANES), F32),
        ],
        compiler_params=_params("parallel", "parallel", "arbitrary"),
        name="sb_prompt",
    )(bias, q, k, v, later, ones, bd, g_sb)


def _sb_sample_kernel(pt_ref, qx_ref, bias_ref, kn_ref, vn_ref, kp_ref, vp_ref, later_ref, ones_ref,
                      hm_ref, bd_ref, g_ref, o_ref, kb_ref, vb_ref, carry_ref, acc_ref, *, n_new, heads):
    del pt_ref
    pg = pl.program_id(1)
    qx = qx_ref[0]
    bias = bias_ref[...]
    rows = qx.shape[0]

    @pl.when(pg == 0)
    def _():
        carry_ref[...] = jnp.zeros_like(carry_ref)
        acc_ref[...] = jnp.zeros_like(acc_ref)
        kb_ref[...] = jnp.zeros_like(kb_ref)
        vb_ref[...] = jnp.zeros_like(vb_ref)
        kb_ref[0:n_new, :] = kn_ref[0].astype(BF16)
        vb_ref[0:n_new, :] = vn_ref[0].astype(BF16)
        t = lax.broadcasted_iota(jnp.int32, (rows, PAGE_SIZE), 0) // heads
        s = lax.broadcasted_iota(jnp.int32, (rows, PAGE_SIZE), 1)
        _sb_tile(qx, kb_ref[...], vb_ref[...], bias, later_ref, ones_ref, carry_ref, acc_ref, s < t)

    _sb_tile(qx, kp_ref[0].astype(BF16), vp_ref[0].astype(BF16), bias,
             later_ref, ones_ref, carry_ref, acc_ref, None)

    @pl.when(pg == pl.num_programs(1) - 1)
    def _():
        a = acc_ref[...] * hm_ref[...]
        o = jnp.sum(a.reshape(rows // heads, heads, a.shape[1]), axis=1)
        ms = _dot_exact_rhs(o * o, bd_ref[...]) * (1.0 / HEAD_DIM)
        o_ref[0] = (o * lax.rsqrt(ms + NORM_EPS) * g_ref[...]).astype(BF16)


def sb_sample(q, k_new, v_new, bias, g_sb, cache_k, cache_v, page_table):
    b, t, w = q.shape
    heads = w // HEAD_DIM
    tpad = 8
    n_pages = page_table.shape[1]
    rows = tpad * heads
    hid = jnp.arange(w) // HEAD_DIM
    head_of_row = jnp.arange(rows) % heads
    hmask = (head_of_row[:, None] == hid[None, :])
    qp = jnp.pad(q, ((0, 0), (0, tpad - t), (0, 0)))
    qx = jnp.where(hmask[None], jnp.repeat(qp, heads, axis=1), jnp.zeros((), BF16))
    bias_full = jnp.broadcast_to(bias[head_of_row][:, None], (rows, PAGE_SIZE)).astype(F32)
    idx = jnp.arange(PAGE_SIZE)
    later = (idx[:, None] > idx[None, :]).astype(BF16)
    ones = jnp.ones((PAGE_SIZE, PAGE_SIZE), BF16)
    bd = _head_block_ones(w)
    kern = functools.partial(_sb_sample_kernel, n_new=t, heads=heads)
    c2 = lambda i, p, pt: (0, 0)
    page = lambda i, p, pt: (pt[i, n_pages - 1 - p], 0, 0)
    grid_spec = pltpu.PrefetchScalarGridSpec(
        num_scalar_prefetch=1,
        grid=(b, n_pages),
        in_specs=[
            pl.BlockSpec((1, rows, w), lambda i, p, pt: (i, 0, 0)),
            pl.BlockSpec((rows, PAGE_SIZE), c2),
            pl.BlockSpec((1, t, w), lambda i, p, pt: (i, 0, 0)),
            pl.BlockSpec((1, t, w), lambda i, p, pt: (i, 0, 0)),
            pl.BlockSpec((1, PAGE_SIZE, w), page),
            pl.BlockSpec((1, PAGE_SIZE, w), page),
            pl.BlockSpec((PAGE_SIZE, PAGE_SIZE), c2),
            pl.BlockSpec((PAGE_SIZE, PAGE_SIZE), c2),
            pl.BlockSpec((rows, w), c2),
            pl.BlockSpec((w, w), c2),
            pl.BlockSpec((1, w), c2),
        ],
        out_specs=pl.BlockSpec((1, tpad, w), lambda i, p, pt: (i, 0, 0)),
        scratch_shapes=[
            pltpu.VMEM((PAGE_SIZE, w), BF16),
            pltpu.VMEM((PAGE_SIZE, w), BF16),
            pltpu.VMEM((rows, PAGE_SIZE), F32),
            pltpu.VMEM((rows, w), F32),
        ],
    )
    return pl.pallas_call(
        kern,
        grid_spec=grid_spec,
        out_shape=jax.ShapeDtypeStruct((b, tpad, w), BF16),
        compiler_params=_params("parallel", "arbitrary"),
        name="sb_sample",
    )(page_table, qx, bias_full, k_new, v_new, cache_k, cache_v, later, ones,
      hmask.astype(F32), bd, g_sb)


def _rwkv_prep_kernel(*refs, rw, halo_mode, tiles_per_seq):
    if halo_mode:
        u_ref, halo_ref, st_ref = refs[:3]
        rest = refs[3:]
    else:
        u_ref, up_ref = refs[:2]
        rest = refs[2:]
    (mu_ref, w0_ref, w2h_ref, w2l_ref, a0_ref, a2h_ref, a2l_ref, g2h_ref, g2l_ref,
     kkw_ref, ka_ref, rk_ref, bd_ref,
     r_ref, k_ref, v_ref, lw_ref, kk_ref, b_ref, gate_ref, bonus_ref) = rest
    u = u_ref[...]
    if halo_mode:
        tm = u.shape[0]
        row = lax.broadcasted_iota(jnp.int32, (tm, 1), 0)
        first = (pl.program_id(0) % tiles_per_seq) == 0
        prev_row = jnp.where(first, st_ref[0], halo_ref[7:8, :])
        up = jnp.where(row == 0, prev_row, pltpu.roll(u, 1, axis=0))
    else:
        up = up_ref[...]
    us = u + (up - u) * mu_ref[...]
    r = us[:, 0:rw]
    k = us[:, rw:2 * rw]
    v = us[:, 2 * rw:3 * rw]
    o = 3 * rw
    w_lo = us[:, o:o + LANES]
    a_lo = us[:, o + LANES:o + 2 * LANES]
    g_lo = us[:, o + 2 * LANES:o + 4 * LANES]
    d = w0_ref[...] + _dot3(jnp.tanh(w_lo), w2h_ref[...], w2l_ref[...])
    w_log = -_softplus(-d) - 0.5
    lw_ref[...] = -jnp.exp(w_log)
    a = _sigmoid(a0_ref[...] + _dot3(a_lo, a2h_ref[...], a2l_ref[...]))
    gate_ref[...] = _dot3(_sigmoid(g_lo), g2h_ref[...], g2l_ref[...])
    kkr = k * kkw_ref[...]
    nrm = jnp.sqrt(_dot_exact_rhs(kkr * kkr, bd_ref[...]))
    kk = kkr / jnp.maximum(nrm, KK_EPS)
    k_eff = k * (1.0 + (a - 1.0) * ka_ref[...])
    r_ref[...] = r
    k_ref[...] = k_eff
    v_ref[...] = v
    kk_ref[...] = kk
    b_ref[...] = kk * a
    bonus_ref[...] = _dot_exact_rhs(r * k_eff * rk_ref[...], bd_ref[...]) * v


def rwkv_prep(u, prev, pw, rw, seq_len, *, tm, halo_mode):
    n, uw = u.shape
    tm = min(tm, n)
    c = lambda i: (0, 0)
    row_spec = pl.BlockSpec((tm, uw), lambda i: (i, 0))
    if halo_mode:
        tps = seq_len // tm
        halo_blk = tm // 8
        lead_specs = [row_spec,
                      pl.BlockSpec((8, uw), lambda i: (jnp.maximum(i * halo_blk - 1, 0), 0)),
                      pl.BlockSpec((1, 1, uw), lambda i: (i // tps, 0, 0))]
        lead = (u, u, prev)
    else:
        tps = 1
        lead_specs = [row_spec, row_spec]
        lead = (u, prev)
    vec = lambda width: pl.BlockSpec((1, width), c)
    mat = lambda rows: pl.BlockSpec((rows, rw), c)
    in_specs = lead_specs + [
        vec(uw), vec(rw), mat(LANES), mat(LANES), vec(rw), mat(LANES), mat(LANES),
        mat(2 * LANES), mat(2 * LANES), vec(rw), vec(rw), vec(rw), pl.BlockSpec((rw, rw), c)]
    out_spec = pl.BlockSpec((tm, rw), lambda i: (i, 0))
    kern = functools.partial(_rwkv_prep_kernel, rw=rw, halo_mode=halo_mode, tiles_per_seq=tps)
    return pl.pallas_call(
        kern,
        grid=(n // tm,),
        in_specs=in_specs,
        out_specs=[out_spec] * 8,
        out_shape=[jax.ShapeDtypeStruct((n, rw), F32)] * 8,
        compiler_params=_params("parallel"),
        name="rwkv_prep",
    )(*lead, pw['mu'], pw['w0'], pw['w2h'], pw['w2l'], pw['a0'], pw['a2h'], pw['a2l'],
      pw['g2h'], pw['g2l'], pw['kk'], pw['ka'], pw['rk'], pw['bd'])


def _rwkv_chunk_kernel(r_ref, k_ref, v_ref, lw_ref, kk_ref, b_ref, y_ref, s_out_ref, s_ref, *, chunk, npairs):
    c = pl.program_id(1)
    C = chunk

    @pl.when(c == 0)
    def _():
        s_ref[...] = jnp.zeros_like(s_ref)

    ti = lax.broadcasted_iota(jnp.int32, (C, C), 0)
    tj = lax.broadcasted_iota(jnp.int32, (C, C), 1)
    incl = ti >= tj
    strict = ti > tj
    eye = (ti == tj).astype(F32)
    lane = lax.broadcasted_iota(jnp.int32, (1, LANES), 1)
    bi = lax.broadcasted_iota(jnp.int32, (LANES, LANES), 0) // HEAD_DIM
    bj = lax.broadcasted_iota(jnp.int32, (LANES, LANES), 1) // HEAD_DIM
    same_head = bi == bj

    lw = lw_ref[...]
    l_hi, l_mid, l_lo = _split3(lw)
    tri = incl.astype(BF16)
    cl = _dot(tri, l_hi) + (_dot(tri, l_mid) + _dot(tri, l_lo))
    cl_end = cl[C - 1:C, :]
    e_pos = jnp.exp(cl)
    e_neg = jnp.exp(-cl)
    e_prev = jnp.exp(cl - lw)
    e_end = jnp.exp(cl_end - cl)
    d_end = jnp.exp(cl_end)
    r = r_ref[...]
    k = k_ref[...]
    b = b_ref[...]
    rt = (r * e_pos).astype(BF16)
    kt = (kk_ref[...] * e_prev).astype(BF16)
    k_inv = (k * e_neg).astype(BF16)
    b_inv = (b * e_neg).astype(BF16)
    k_dec = (k * e_end).astype(BF16)
    b_dec = (b * e_end).astype(BF16)
    vb = v_ref[...].astype(BF16)

    for p in range(npairs):
        sl = slice(p * LANES, (p + 1) * LANES)
        s0 = s_ref[p]
        s0b = s0.astype(BF16)
        kt_p, rt_p, v_p = kt[:, sl], rt[:, sl], vb[:, sl]
        ks = _dot_nt(kt_p, s0b)
        rs = _dot_nt(rt_p, s0b)
        u_heads, y_heads = [], []
        for hh in range(HEADS_PER_TILE):
            in_head = (lane // HEAD_DIM) == hh
            zero = jnp.zeros_like(kt_p)
            lhs = jnp.concatenate([jnp.where(in_head, kt_p, zero), jnp.where(in_head, rt_p, zero)], axis=0)
            ab_k = _dot_nt(lhs, k_inv[:, sl])
            ab_b = _dot_nt(lhs, b_inv[:, sl])
            a_m = jnp.where(strict, ab_k[:C], 0.0)
            aq_m = jnp.where(incl, ab_k[C:], 0.0)
            n_m = jnp.where(strict, -ab_b[:C], 0.0)
            bq_m = jnp.where(incl, ab_b[C:], 0.0)
            g = eye + n_m
            pw = n_m
            for _ in range(int(math.log2(C)) - 1):
                pwb = pw.astype(BF16)
                pw = _dot(pwb, pwb)
                g = g + _dot(g.astype(BF16), pw.astype(BF16))
            x = ks + _dot(a_m.astype(BF16), v_p)
            u_h = _dot(g.astype(BF16), x.astype(BF16))
            u_heads.append(u_h)
            y_heads.append((aq_m, bq_m))
        u_p = jnp.where((lane // HEAD_DIM) == 0, u_heads[0], u_heads[1])
        u_pb = u_p.astype(BF16)
        ys = []
        for hh in range(HEADS_PER_TILE):
            aq_m, bq_m = y_heads[hh]
            ys.append(rs + _dot(aq_m.astype(BF16), v_p) - _dot(bq_m.astype(BF16), u_pb))
        y_ref[:, sl] = jnp.where((lane // HEAD_DIM) == 0, ys[0], ys[1])
        upd = _dot_tn(jnp.concatenate([v_p, -u_pb], axis=0),
                      jnp.concatenate([k_dec[:, sl], b_dec[:, sl]], axis=0))
        s_new = s0 * d_end[:, sl] + jnp.where(same_head, upd, 0.0)
        s_ref[p] = s_new

        @pl.when(c == pl.num_programs(1) - 1)
        def _():
            s_out_ref[0, p] = s_new


def rwkv_chunk(r, k, v, lw, kk, b, batch, seq_len):
    n, rw = r.shape
    npairs = rw // LANES
    C = RW_CHUNK
    nc = seq_len // C
    blk = pl.BlockSpec((C, rw), lambda i, c: (i * nc + c, 0))
    kern = functools.partial(_rwkv_chunk_kernel, chunk=C, npairs=npairs)
    return pl.pallas_call(
        kern,
        grid=(batch, nc),
        in_specs=[blk] * 6,
        out_specs=[blk, pl.BlockSpec((1, npairs, LANES, LANES), lambda i, c: (i, 0, 0, 0))],
        out_shape=[jax.ShapeDtypeStruct((n, rw), F32),
                   jax.ShapeDtypeStruct((batch, npairs, LANES, LANES), F32)],
        scratch_shapes=[pltpu.VMEM((npairs, LANES, LANES), F32)],
        compiler_params=_params("parallel", "arbitrary"),
        name="rwkv_chunk",
    )(r, k, v, lw, kk, b)


def _rwkv_seq_kernel(s0_ref, r_ref, k_ref, lw_ref, kk_ref, b_ref, vcol_ref, ycol_ref, s_out_ref, *, steps):
    s = s0_ref[0]
    for t in range(steps):
        sa = jnp.sum(s * kk_ref[0, t], axis=-1, keepdims=True)
        s = s * jnp.exp(lw_ref[0, t]) - sa * b_ref[0, t] + vcol_ref[0, t] * k_ref[0, t]
        ycol_ref[0, t] = jnp.sum(s * r_ref[0, t], axis=-1, keepdims=True)
    s_out_ref[0] = s


def rwkv_seq(s0, r, k, v, lw, kk, b):
    bsz, heads, dh, _ = s0.shape
    t = r.shape[1]
    rowv = lambda x: x.reshape(bsz, t, heads, 1, dh)
    row_spec = pl.BlockSpec((1, t, heads, 1, dh), lambda i: (i, 0, 0, 0, 0))
    col_spec = pl.BlockSpec((1, t, heads, dh, 1), lambda i: (i, 0, 0, 0, 0))
    st_spec = pl.BlockSpec((1, heads, dh, dh), lambda i: (i, 0, 0, 0))
    kern = functools.partial(_rwkv_seq_kernel, steps=t)
    ycol, s = pl.pallas_call(
        kern,
        grid=(bsz,),
        in_specs=[st_spec] + [row_spec] * 5 + [col_spec],
        out_specs=[col_spec, st_spec],
        out_shape=[jax.ShapeDtypeStruct((bsz, t, heads, dh, 1), F32),
                   jax.ShapeDtypeStruct(s0.shape, F32)],
        compiler_params=_params("parallel"),
        name="rwkv_seq",
    )(s0, rowv(r), rowv(k), rowv(lw), rowv(kk), rowv(b), v.reshape(bsz, t, heads, dh, 1))
    return ycol.reshape(bsz, t, heads * dh), s


def _rwkv_epi_kernel(y_ref, bonus_ref, gate_ref, g_ref, b_ref, bd_ref, o_ref):
    y = y_ref[...]
    inv = 1.0 / HEAD_DIM
    mu = _dot_exact_rhs(y, bd_ref[...]) * inv
    yc = y - mu
    var = _dot_exact_rhs(yc * yc, bd_ref[...]) * inv
    yn = yc * lax.rsqrt(var + LNX_EPS) * g_ref[...] + b_ref[...]
    o_ref[...] = ((yn + bonus_ref[...]) * gate_ref[...]).astype(BF16)


def rwkv_epilogue(y, bonus, gate, ln_g, ln_b, bd, *, tm):
    n, rw = y.shape
    tm = min(tm, n)
    blk = pl.BlockSpec((tm, rw), lambda i: (i, 0))
    vec = pl.BlockSpec((1, rw), lambda i: (0, 0))
    return pl.pallas_call(
        _rwkv_epi_kernel,
        grid=(n // tm,),
        in_specs=[blk, blk, blk, vec, vec, pl.BlockSpec((rw, rw), lambda i: (0, 0))],
        out_specs=blk,
        out_shape=jax.ShapeDtypeStruct((n, rw), BF16),
        compiler_params=_params("parallel"),
        name="rwkv_epilogue",
    )(y, bonus, gate, ln_g, ln_b, bd)


def _pad_rows(m, rows):
    return jnp.pad(m, ((0, rows - m.shape[0]), (0, 0)))


def _hi_lo(m):
    hi = m.astype(BF16)
    return hi, (m - hi.astype(F32)).astype(BF16)


def _u_layout(rw, n_decay, n_icl, n_gate):
    segs = [(3 * rw, 3 * rw), (n_decay, LANES), (n_icl, LANES), (n_gate, 2 * LANES)]
    assert n_decay <= LANES and n_icl <= LANES and n_gate <= 2 * LANES
    return segs


def _pad_u_cols(m, segs):
    out, o = [], 0
    for width, padded in segs:
        out.append(m[..., o:o + width])
        if padded > width:
            out.append(jnp.zeros(m.shape[:-1] + (padded - width,), m.dtype))
        o += width
    return jnp.concatenate(out, axis=-1)


def _unpad_u_cols(m, segs):
    out, o = [], 0
    for width, padded in segs:
        out.append(m[..., o:o + width])
        o += padded
    return jnp.concatenate(out, axis=-1)


def _prepare_weights(w_in, w_out, g_pre_mix, g_post_mix, g_pre_ffn, g_post_ffn, g_sb_out, sb_bias,
                     rw_mu, rw_w0, rw_w2, rw_a0, rw_a2, rw_g2, rw_kk, rw_ka, rw_rk, rw_ln_g, rw_ln_b,
                     w_up, conv_w, conv_b, w_down):
    rw = rw_w0.shape[0]
    sbw = g_sb_out.shape[0]
    segs = _u_layout(rw, rw_w2.shape[0], rw_a2.shape[0], rw_g2.shape[0])
    row = lambda x: x.reshape(1, -1)
    w2h, w2l = _hi_lo(_pad_rows(rw_w2, LANES))
    a2h, a2l = _hi_lo(_pad_rows(rw_a2, LANES))
    g2h, g2l = _hi_lo(_pad_rows(rw_g2, 2 * LANES))
    bd = _head_block_ones(rw)
    w_in_p = jnp.concatenate([w_in[:, :3 * sbw], _pad_u_cols(w_in[:, 3 * sbw:], segs)], axis=1).astype(BF16)
    prep = dict(mu=row(_pad_u_cols(rw_mu, segs)), w0=row(rw_w0), w2h=w2h, w2l=w2l, a0=row(rw_a0),
                a2h=a2h, a2l=a2l, g2h=g2h, g2l=g2l, kk=row(rw_kk), ka=row(rw_ka),
                rk=row(rw_rk), bd=bd)
    return dict(
        segs=segs, rw=rw, sbw=sbw, w_in=w_in_p, w_out=w_out.astype(BF16),
        g_pre_mix=row(g_pre_mix), g_post_mix=row(g_post_mix), g_pre_ffn=row(g_pre_ffn),
        g_post_ffn=row(g_post_ffn), g_sb=row(g_sb_out), sb_bias=sb_bias, prep=prep,
        ln_g=row(rw_ln_g), ln_b=row(rw_ln_b), bd=bd,
        w_up=w_up.astype(BF16), conv_w=conv_w, conv_b=row(conv_b), w_down=w_down.astype(BF16))


def _tail(wts, x, o_sb, o_rw, *, tm):
    mixed_in = jnp.concatenate([o_sb, o_rw], axis=-1)
    return mm_norm_resid(mixed_in, wts['w_out'], x, wts['g_post_mix'], wts['g_pre_ffn'],
                         tm=tm, tk=mixed_in.shape[1], emit_h=True)


def _down(wts, act, x1, *, tm):
    (y,) = mm_norm_resid(act, wts['w_down'], x1, wts['g_post_ffn'], wts['g_post_ffn'],
                         tm=tm, tk=act.shape[1] // 4, emit_h=False)
    return y


def _prompt_layer(wts, x3):
    bsz, t, d = x3.shape
    n = bsz * t
    x = x3.reshape(n, d)
    rw, sbw, segs = wts['rw'], wts['sbw'], wts['segs']
    uw = sum(p for _, p in segs)
    q, k, v, u = in_proj(x, wts['g_pre_mix'], wts['w_in'], sbw, uw, tm=512)
    o_sb = sb_prompt(q, k, v, wts['sb_bias'], wts['g_sb'], bsz, t, tq=min(256, t))
    shift0 = jnp.zeros((bsz, 1, uw), F32)
    r, ke, vr, lw, kk, b, gate, bonus = rwkv_prep(u, shift0, wts['prep'], rw, t, tm=min(256, t), halo_mode=True)
    y, s_pairs = rwkv_chunk(r, ke, vr, lw, kk, b, bsz, t)
    o_rw = rwkv_epilogue(y, bonus, gate, wts['ln_g'], wts['ln_b'], wts['bd'], tm=512)
    x1, h2 = _tail(wts, x, o_sb, o_rw, tm=512)
    tm_f = min(512, t)
    conv0 = jnp.zeros((bsz, CONV_W - 1, wts['conv_w'].shape[1]), F32)
    act, gtail = ffn_up_prompt(h2, wts['w_up'], wts['conv_w'], wts['conv_b'], conv0, t, tm=tm_f)
    yout = _down(wts, act, x1, tm=512)
    heads = rw // HEAD_DIM
    hp = HEADS_PER_TILE
    sp = s_pairs.reshape(bsz, rw // LANES, hp, HEAD_DIM, hp, HEAD_DIM)
    wkv = jnp.stack([sp[:, :, i, :, i, :] for i in range(hp)], axis=2).reshape(bsz, heads, HEAD_DIM, HEAD_DIM)
    shift = _unpad_u_cols(u.reshape(bsz, t, uw)[:, -1], segs)
    tps = t // tm_f
    conv_new = gtail.reshape(bsz, tps, 8, -1)[:, -1, 8 - (CONV_W - 1):]
    kvshape = (bsz, t, sbw // HEAD_DIM, HEAD_DIM)
    return yout.reshape(bsz, t, d), k.reshape(kvshape), v.reshape(kvshape), wkv, shift, conv_new


def _sample_layer(wts, x3, cache_k, cache_v, page_table, wkv0, shift_prev, conv_prev):
    bsz, t, d = x3.shape
    n = bsz * t
    x = x3.reshape(n, d)
    rw, sbw, segs = wts['rw'], wts['sbw'], wts['segs']
    uw = sum(p for _, p in segs)
    q, k, v, u = in_proj(x, wts['g_pre_mix'], wts['w_in'], sbw, uw, tm=n)
    pool = cache_k.shape[0]
    o_sb8 = sb_sample(q.reshape(bsz, t, sbw), k.reshape(bsz, t, sbw), v.reshape(bsz, t, sbw),
                      wts['sb_bias'], wts['g_sb'], cache_k.reshape(pool, PAGE_SIZE, sbw),
                      cache_v.reshape(pool, PAGE_SIZE, sbw), page_table)
    o_sb = o_sb8[:, :t].reshape(n, sbw)
    u3 = u.reshape(bsz, t, uw)
    u_prev = jnp.concatenate([_pad_u_cols(shift_prev, segs)[:, None], u3[:, :-1]], axis=1).reshape(n, uw)
    r, ke, vr, lw, kk, b, gate, bonus = rwkv_prep(u, u_prev, wts['prep'], rw, t, tm=n, halo_mode=False)
    as3 = lambda a: a.reshape(bsz, t, rw)
    y3, wkv = rwkv_seq(wkv0, as3(r), as3(ke), as3(vr), as3(lw), as3(kk), as3(b))
    o_rw = rwkv_epilogue(y3.reshape(n, rw), bonus, gate, wts['ln_g'], wts['ln_b'], wts['bd'], tm=n)
    x1, h2 = _tail(wts, x, o_sb, o_rw, tm=n)
    f = wts['conv_w'].shape[1]
    zeros = jnp.zeros((bsz, t, f), F32)
    p1 = zeros.at[:, 0].set(conv_prev[:, 1]).reshape(n, f)
    p2 = zeros.at[:, 0].set(conv_prev[:, 0]).at[:, 1].set(conv_prev[:, 1]).reshape(n, f)
    act, gfull = ffn_up_sample(h2, wts['w_up'], wts['conv_w'], wts['conv_b'], p1, p2, t)
    yout = _down(wts, act, x1, tm=n)
    shift = _unpad_u_cols(u3[:, -1], segs)
    conv_new = gfull.reshape(bsz, t, f)[:, t - (CONV_W - 1):]
    kvshape = (bsz, t, sbw // HEAD_DIM, HEAD_DIM)
    return yout.reshape(bsz, t, d), k.reshape(kvshape), v.reshape(kvshape), wkv, shift, conv_new


def kernel(x_prompt, x_sample, cache_sb_k, cache_sb_v, state_rwkv_wkv, state_rwkv_shift, state_ffn_conv, page_table, w_in, w_out, g_pre_mix, g_post_mix, g_pre_ffn, g_post_ffn, g_sb_out, sb_bias, rw_mu, rw_w0, rw_w2, rw_a0, rw_a2, rw_g2, rw_kk, rw_ka, rw_rk, rw_ln_g, rw_ln_b, w_up, conv_w, conv_b, w_down):
    depth = w_in.shape[0]
    yp, ys = x_prompt, x_sample
    outs_p, outs_s = [], []
    for l in range(depth):
        wts = _prepare_weights(w_in[l], w_out[l], g_pre_mix[l], g_post_mix[l], g_pre_ffn[l], g_post_ffn[l],
                               g_sb_out[l], sb_bias[l], rw_mu[l], rw_w0[l], rw_w2[l], rw_a0[l], rw_a2[l],
                               rw_g2[l], rw_kk[l], rw_ka[l], rw_rk[l].reshape(-1), rw_ln_g[l], rw_ln_b[l],
                               w_up[l], conv_w[l], conv_b[l], w_down[l])
        yp, *rest_p = _prompt_layer(wts, yp)
        ys, *rest_s = _sample_layer(wts, ys, cache_sb_k[l], cache_sb_v[l], page_table,
                                    state_rwkv_wkv[l], state_rwkv_shift[l], state_ffn_conv[l])
        outs_p.append(rest_p)
        outs_s.append(rest_s)
    stack = lambda outs, i: jnp.stack([o[i] for o in outs])
    return (yp, ys,
            stack(outs_p, 0), stack(outs_p, 1), stack(outs_p, 2), stack(outs_p, 3), stack(outs_p, 4),
            stack(outs_s, 0), stack(outs_s, 1), stack(outs_s, 2), stack(outs_s, 3), stack(outs_s, 4))
```

```python
import functools
import math

import jax
import jax.numpy as jnp
from jax import lax
from jax.experimental import pallas as pl
from jax.experimental.pallas import tpu as pltpu

F32 = jnp.float32
BF16 = jnp.bfloat16

HEAD_DIM = 64
LANES = 128
SUBLANES = 8
HEADS_PER_TILE = LANES // HEAD_DIM
NORM_EPS = 1e-6
LNX_EPS = 64e-5
KK_EPS = 1e-12
CONV_W = 3
PAGE_SIZE = 128
VMEM_LIMIT = 48 * 1024 * 1024
RW_CHUNK = 64


def _dot(a, b):
    return jnp.dot(a, b, preferred_element_type=F32)


def _dot_nt(a, b):
    return lax.dot_general(a, b, (((1,), (1,)), ((), ())), preferred_element_type=F32)


def _dot_tn(a, b):
    return lax.dot_general(a, b, (((0,), (0,)), ((), ())), preferred_element_type=F32)


def _split2(x):
    hi = x.astype(BF16)
    lo = (x - hi.astype(F32)).astype(BF16)
    return hi, lo


def _split3(x):
    hi = x.astype(BF16)
    r1 = x - hi.astype(F32)
    mid = r1.astype(BF16)
    lo = (r1 - mid.astype(F32)).astype(BF16)
    return hi, mid, lo


def _dot_exact_rhs(x, m01):
    hi, lo = _split2(x)
    return _dot(hi, m01) + _dot(lo, m01)


def _dot3(a, b_hi, b_lo):
    a_hi, a_lo = _split2(a)
    return _dot(a_hi, b_hi) + (_dot(a_lo, b_hi) + _dot(a_hi, b_lo))


def _softplus(x):
    return jnp.maximum(x, 0.0) + jnp.log(1.0 + jnp.exp(-jnp.abs(x)))


def _sigmoid(x):
    return 1.0 / (1.0 + jnp.exp(-x))


def _rms_rows(x, g):
    return x * lax.rsqrt(jnp.mean(x * x, axis=-1, keepdims=True) + NORM_EPS) * g


def _params(*sem):
    return pltpu.CompilerParams(dimension_semantics=sem, vmem_limit_bytes=VMEM_LIMIT)


def _head_block_ones(n):
    i = jnp.arange(n) // HEAD_DIM
    return (i[:, None] == i[None, :]).astype(BF16)


def _in_proj_kernel(x_ref, g_ref, w_ref, q_ref, k_ref, v_ref, u_ref, h_ref, *, nq, nk, nv):
    j = pl.program_id(1)

    @pl.when(j == 0)
    def _():
        h_ref[...] = _rms_rows(x_ref[...], g_ref[...]).astype(BF16)

    res = _dot(h_ref[...], w_ref[...])

    @pl.when(j < nq)
    def _():
        q_ref[...] = (res * (HEAD_DIM ** -0.5)).astype(BF16)

    @pl.when((j >= nq) & (j < nq + nk))
    def _():
        k_ref[...] = res

    @pl.when((j >= nq + nk) & (j < nq + nk + nv))
    def _():
        v_ref[...] = res

    @pl.when(j >= nq + nk + nv)
    def _():
        u_ref[...] = res


def in_proj(x, g, w, sb_width, u_width, *, tm, tn=512):
    n, d = x.shape
    tm = min(tm, n)
    nq = nk = nv = sb_width // tn
    nu = u_width // tn
    nj = nq + nk + nv + nu
    off_k, off_v, off_u = nq, nq + nk, nq + nk + nv
    clip = lambda j, lo, cnt: jnp.clip(j - lo, 0, cnt - 1)
    kern = functools.partial(_in_proj_kernel, nq=nq, nk=nk, nv=nv)
    return pl.pallas_call(
        kern,
        grid=(n // tm, nj),
        in_specs=[
            pl.BlockSpec((tm, d), lambda i, j: (i, 0)),
            pl.BlockSpec((1, d), lambda i, j: (0, 0)),
            pl.BlockSpec((d, tn), lambda i, j: (0, j)),
        ],
        out_specs=[
            pl.BlockSpec((tm, tn), lambda i, j: (i, clip(j, 0, nq))),
            pl.BlockSpec((tm, tn), lambda i, j: (i, clip(j, off_k, nk))),
            pl.BlockSpec((tm, tn), lambda i, j: (i, clip(j, off_v, nv))),
            pl.BlockSpec((tm, tn), lambda i, j: (i, clip(j, off_u, nu))),
        ],
        out_shape=[
            jax.ShapeDtypeStruct((n, sb_width), BF16),
            jax.ShapeDtypeStruct((n, sb_width), F32),
            jax.ShapeDtypeStruct((n, sb_width), F32),
            jax.ShapeDtypeStruct((n, u_width), F32),
        ],
        scratch_shapes=[pltpu.VMEM((tm, d), BF16)],
        compiler_params=_params("parallel", "arbitrary"),
        name="in_proj",
    )(x, g, w)


def _mm_norm_resid_kernel(a_ref, w_ref, r_ref, g_ref, g2_ref, *rest, nk, emit_h):
    if emit_h:
        o_ref, h_ref, acc_ref = rest
    else:
        o_ref, acc_ref = rest
        h_ref = None
    kk = pl.program_id(1)
    part = _dot(a_ref[...], w_ref[...])

    def finalize(m):
        y = r_ref[...] + _rms_rows(m, g_ref[...])
        o_ref[...] = y
        if emit_h:
            h_ref[...] = _rms_rows(y, g2_ref[...]).astype(BF16)

    if nk == 1:
        finalize(part)
    else:
        @pl.when(kk == 0)
        def _():
            acc_ref[...] = part

        @pl.when((kk > 0) & (kk < nk - 1))
        def _():
            acc_ref[...] += part

        @pl.when(kk == nk - 1)
        def _():
            finalize(acc_ref[...] + part)


def mm_norm_resid(a, w, resid, g, g2, *, tm, tk, emit_h):
    n, kdim = a.shape
    d = w.shape[1]
    tm = min(tm, n)
    nk = kdim // tk
    kern = functools.partial(_mm_norm_resid_kernel, nk=nk, emit_h=emit_h)
    out_specs = [pl.BlockSpec((tm, d), lambda i, k: (i, 0))]
    out_shape = [jax.ShapeDtypeStruct((n, d), F32)]
    if emit_h:
        out_specs.append(pl.BlockSpec((tm, d), lambda i, k: (i, 0)))
        out_shape.append(jax.ShapeDtypeStruct((n, d), BF16))
    return pl.pallas_call(
        kern,
        grid=(n // tm, nk),
        in_specs=[
            pl.BlockSpec((tm, tk), lambda i, k: (i, k)),
            pl.BlockSpec((tk, d), lambda i, k: (k, 0)),
            pl.BlockSpec((tm, d), lambda i, k: (i, 0)),
            pl.BlockSpec((1, d), lambda i, k: (0, 0)),
            pl.BlockSpec((1, d), lambda i, k: (0, 0)),
        ],
        out_specs=out_specs,
        out_shape=out_shape,
        scratch_shapes=[pltpu.VMEM((tm, d), F32)],
        compiler_params=_params("parallel", "arbitrary"),
        name="mm_norm_resid",
    )(a, w, resid, g, g2)


def _gelu_tanh(x):
    c = math.sqrt(2.0 / math.pi)
    return x * (0.5 * (1.0 + jnp.tanh(c * (x + 0.044715 * (x * x * x)))))


def _ffn_up_kernel(*refs, tiles_per_seq, seq_len, halo_mode):
    if halo_mode:
        h_ref, halo_ref, wg_ref, wv_ref, cw_ref, cb_ref, prev_ref, act_ref, g_out_ref = refs
    else:
        h_ref, wg_ref, wv_ref, cw_ref, cb_ref, p1_ref, p2_ref, act_ref, g_out_ref = refs
    h = h_ref[...]
    g = _dot(h, wg_ref[...])
    val = _dot(h, wv_ref[...])
    tm = g.shape[0]
    row = lax.broadcasted_iota(jnp.int32, (tm, 1), 0)
    r1 = pltpu.roll(g, 1, axis=0)
    r2 = pltpu.roll(g, 2, axis=0)
    if halo_mode:
        gh = _dot(halo_ref[...], wg_ref[...])
        first = (pl.program_id(0) % tiles_per_seq) == 0
        pm1 = jnp.where(first, prev_ref[0, 1:2, :], gh[7:8, :])
        pm2 = jnp.where(first, prev_ref[0, 0:1, :], gh[6:7, :])
        g1 = jnp.where(row == 0, pm1, r1)
        g2 = jnp.where(row == 0, pm2, jnp.where(row == 1, pm1, r2))
        g_out_ref[0] = g[tm - 8:, :]
    else:
        t = row % seq_len
        g1 = jnp.where(t >= 1, r1, p1_ref[...])
        g2 = jnp.where(t >= 2, r2, p2_ref[...])
        g_out_ref[...] = g
    conv = cb_ref[...] + cw_ref[0:1, :] * g2 + cw_ref[1:2, :] * g1 + cw_ref[2:3, :] * g
    act_ref[...] = (_gelu_tanh(conv) * val).astype(BF16)


def ffn_up_prompt(h, w_up, conv_w, conv_b, conv_prev, seq_len, *, tm, tn=512):
    n, d = h.shape
    f = w_up.shape[1] // 2
    nf = f // tn
    tps = seq_len // tm
    kern = functools.partial(_ffn_up_kernel, tiles_per_seq=tps, seq_len=seq_len, halo_mode=True)
    halo_blk = tm // 8
    return pl.pallas_call(
        kern,
        grid=(n // tm, nf),
        in_specs=[
            pl.BlockSpec((tm, d), lambda i, j: (i, 0)),
            pl.BlockSpec((8, d), lambda i, j: (jnp.maximum(i * halo_blk - 1, 0), 0)),
            pl.BlockSpec((d, tn), lambda i, j: (0, j)),
            pl.BlockSpec((d, tn), lambda i, j: (0, j + nf)),
            pl.BlockSpec((CONV_W, tn), lambda i, j: (0, j)),
            pl.BlockSpec((1, tn), lambda i, j: (0, j)),
            pl.BlockSpec((1, CONV_W - 1, tn), lambda i, j: (i // tps, 0, j)),
        ],
        out_specs=[
            pl.BlockSpec((tm, tn), lambda i, j: (i, j)),
            pl.BlockSpec((1, 8, tn), lambda i, j: (i, 0, j)),
        ],
        out_shape=[
            jax.ShapeDtypeStruct((n, f), BF16),
            jax.ShapeDtypeStruct((n // tm, 8, f), F32),
        ],
        compiler_params=_params("parallel", "arbitrary"),
        name="ffn_up_prompt",
    )(h, h, w_up, w_up, conv_w, conv_b, conv_prev)


def ffn_up_sample(h, w_up, conv_w, conv_b, p1, p2, seq_len, *, tn=512):
    n, d = h.shape
    f = w_up.shape[1] // 2
    nf = f // tn
    kern = functools.partial(_ffn_up_kernel, tiles_per_seq=1, seq_len=seq_len, halo_mode=False)
    return pl.pallas_call(
        kern,
        grid=(1, nf),
        in_specs=[
            pl.BlockSpec((n, d), lambda i, j: (0, 0)),
            pl.BlockSpec((d, tn), lambda i, j: (0, j)),
            pl.BlockSpec((d, tn), lambda i, j: (0, j + nf)),
            pl.BlockSpec((CONV_W, tn), lambda i, j: (0, j)),
            pl.BlockSpec((1, tn), lambda i, j: (0, j)),
            pl.BlockSpec((n, tn), lambda i, j: (0, j)),
            pl.BlockSpec((n, tn), lambda i, j: (0, j)),
        ],
        out_specs=[
            pl.BlockSpec((n, tn), lambda i, j: (0, j)),
            pl.BlockSpec((n, tn), lambda i, j: (0, j)),
        ],
        out_shape=[
            jax.ShapeDtypeStruct((n, f), BF16),
            jax.ShapeDtypeStruct((n, f), F32),
        ],
        compiler_params=_params("arbitrary", "arbitrary"),
        name="ffn_up_sample",
    )(h, w_up, w_up, conv_w, conv_b, p1, p2)


def _sb_prompt_kernel(bias_ref, q_ref, k_ref, v_ref, later_ref, bd_ref, g_ref, o_ref,
                      kb_ref, vb_ref, carry_ref, acc_ref, *, tq, tk):
    p = pl.program_id(1)
    qi = pl.program_id(2)
    ratio = tq // tk

    @pl.when(qi == 0)
    def _():
        kb_ref[...] = k_ref[...].astype(BF16)
        vb_ref[...] = v_ref[...].astype(BF16)

    q = q_ref[...]
    lane = lax.broadcasted_iota(jnp.int32, (1, LANES), 1)
    qms = [jnp.where((lane // HEAD_DIM) == hh, q, jnp.zeros_like(q)) for hh in range(HEADS_PER_TILE)]
    biases = [bias_ref[p * HEADS_PER_TILE + hh] for hh in range(HEADS_PER_TILE)]
    carry_ref[...] = jnp.zeros_like(carry_ref)
    acc_ref[...] = jnp.zeros_like(acc_ref)
    later = later_ref[...]

    def key_tile(j, masked):
        j0 = pl.multiple_of(j * tk, tk)
        kblk = kb_ref[pl.ds(j0, tk), :]
        vblk = vb_ref[pl.ds(j0, tk), :]
        mask = None
        if masked:
            qpos = qi * tq + lax.broadcasted_iota(jnp.int32, (tq, tk), 0)
            kpos = j0 + lax.broadcasted_iota(jnp.int32, (tq, tk), 1)
            mask = kpos < qpos
        hs = range(HEADS_PER_TILE)
        z = [_dot_nt(qms[hh], kblk) + biases[hh] for hh in hs]
        c = [carry_ref[hh] for hh in hs]
        sp = [_softplus(z[hh]) for hh in hs]
        if mask is not None:
            sp = [jnp.where(mask, s, 0.0) for s in sp]
        ls = [z[hh] - sp[hh] for hh in hs]
        after = [_dot(s.astype(BF16), later) for s in sp]
        w = [jnp.exp(ls[hh] - after[hh] - jnp.concatenate([c[hh]] * (tk // LANES), axis=1)) for hh in hs]
        if mask is not None:
            w = [jnp.where(mask, x, 0.0) for x in w]
        pv = [_dot(x.astype(BF16), vblk) for x in w]
        for hh in hs:
            acc_ref[hh] += pv[hh]
            carry_ref[hh] = c[hh] + jnp.sum(sp[hh], axis=-1, keepdims=True)

    for d in range(ratio - 1, -1, -1):
        key_tile(qi * ratio + d, True)

    def body(it, c):
        key_tile(qi * ratio - 1 - it, False)
        return c

    lax.fori_loop(0, qi * ratio, body, 0)
    o = acc_ref[0]
    for hh in range(1, HEADS_PER_TILE):
        o = jnp.where((lane // HEAD_DIM) == hh, acc_ref[hh], o)
    ms = _dot_exact_rhs(o * o, bd_ref[...]) * (1.0 / HEAD_DIM)
    o_ref[...] = (o * lax.rsqrt(ms + NORM_EPS) * g_ref[...]).astype(BF16)


def _later_matrix(n):
    idx = jnp.arange(n)
    return (idx[:, None] > idx[None, :]).astype(BF16)


def sb_prompt(q, k, v, bias, g_sb, batch, seq_len, *, tq, tk):
    n, w = q.shape
    npairs = w // LANES
    nq = seq_len // tq
    kern = functools.partial(_sb_prompt_kernel, tq=tq, tk=tk)
    const = lambda b, p, i: (0, 0)
    return pl.pallas_call(
        kern,
        grid=(batch, npairs, nq),
        in_specs=[
            pl.BlockSpec(memory_space=pltpu.SMEM),
            pl.BlockSpec((tq, LANES), lambda b, p, i: (b * nq + i, p)),
            pl.BlockSpec((seq_len, LANES), lambda b, p, i: (b, p)),
            pl.BlockSpec((seq_len, LANES), lambda b, p, i: (b, p)),
            pl.BlockSpec((tk, tk), const),
            pl.BlockSpec((LANES, LANES), const),
            pl.BlockSpec((1, LANES), lambda b, p, i: (0, p)),
        ],
        out_specs=pl.BlockSpec((tq, LANES), lambda b, p, i: (b * nq + i, p)),
        out_shape=jax.ShapeDtypeStruct((n, w), BF16),
        scratch_shapes=[
            pltpu.VMEM((seq_len, LANES), BF16),
            pltpu.VMEM((seq_len, LANES), BF16),
            pltpu.VMEM((HEADS_PER_TILE, tq, LANES), F32),
            pltpu.VMEM((HEADS_PER_TILE, tq, LANES), F32),
        ],
        compiler_params=_params("parallel", "parallel", "arbitrary"),
        name="sb_prompt",
    )(bias, q, k, v, _later_matrix(tk), _head_block_ones(LANES), g_sb)


SAMPLE_T_PAD = 8
PAGES_PER_STEP = 4
SAMPLE_HEAD_GROUP = 4


def _sb_sample_kernel(pt_ref, q_ref, bias_ref, g_ref, later_ref, ones_ref, *rest, heads, pps):
    del pt_ref
    nh = heads // SUBLANES
    rows2d = lambda ref: ref.reshape(PAGE_SIZE * SUBLANES, HEAD_DIM)
    split = lambda refs: [[rows2d(r) for r in refs[i * nh:(i + 1) * nh]] for i in range(len(refs) // nh)]
    kn, vn = split(rest[:nh]), split(rest[nh:2 * nh])
    rest = rest[2 * nh:]
    kp, vp = split(rest[:pps * nh]), split(rest[pps * nh:2 * pps * nh])
    o_ref, carry_ref, acc_ref = rest[2 * pps * nh:]
    pg = pl.program_id(1)
    tp = SAMPLE_T_PAD
    rows = heads * tp
    bias = bias_ref[...]
    later = later_ref[...]
    ones = ones_ref[...]

    hg = SAMPLE_HEAD_GROUP
    grows = hg * tp
    groups = range(heads // hg)
    q_grp = [q_ref[0, g * grows:(g + 1) * grows, :].astype(BF16) for g in groups]
    row_head = lax.broadcasted_iota(jnp.int32, (grows, PAGE_SIZE), 0) // tp

    def head_major(page, g):
        hs = range(g * hg, (g + 1) * hg)
        return jnp.concatenate([page[h // SUBLANES][pl.ds(h % SUBLANES, PAGE_SIZE, stride=SUBLANES), :].astype(BF16)
                                for h in hs], axis=0)

    def scores(k_ref):
        out = []
        for g in groups:
            zg = _dot_nt(q_grp[g], head_major(k_ref, g))
            out += [zg[h * tp:(h + 1) * tp, h * PAGE_SIZE:(h + 1) * PAGE_SIZE] for h in range(hg)]
        return jnp.concatenate(out, axis=0)

    def weighted_values(w, v_ref):
        out = []
        for g in groups:
            wb = w[g * grows:(g + 1) * grows].astype(BF16)
            wide = jnp.concatenate([jnp.where(row_head == h, wb, jnp.zeros((), BF16)) for h in range(hg)], axis=1)
            out.append(_dot(wide, head_major(v_ref, g)))
        return jnp.concatenate(out, axis=0)

    def sweep(k_refs, v_refs, carry, acc, mask):
        n = len(k_refs)
        z = [scores(k_refs[i]) + bias for i in range(n)]
        sp = [_softplus(x) for x in z]
        if mask is not None:
            sp = [jnp.where(mask, s, 0.0) for s in sp]
        spb = [s.astype(BF16) for s in sp]
        after = [_dot(s, later) for s in spb]
        total = [_dot(s, ones) for s in spb]
        w = []
        for i in range(n):
            wi = jnp.exp(z[i] - sp[i] - after[i] - carry)
            w.append(wi if mask is None else jnp.where(mask, wi, 0.0))
            carry = carry + total[i]
        pv = [weighted_values(w[i], v_refs[i]) for i in range(n)]
        for i in range(n):
            acc = acc + pv[i]
        return carry, acc

    def store_state(carry, acc):
        carry_ref[...] = carry
        acc_ref[...] = acc

    @pl.when(pg == 0)
    def _():
        t = lax.broadcasted_iota(jnp.int32, (rows, PAGE_SIZE), 0) % tp
        s = lax.broadcasted_iota(jnp.int32, (rows, PAGE_SIZE), 1)
        store_state(*sweep(kn, vn, jnp.zeros((rows, PAGE_SIZE), F32), jnp.zeros((rows, HEAD_DIM), F32), s < t))

    store_state(*sweep(kp, vp, carry_ref[...], acc_ref[...], None))

    @pl.when(pg == pl.num_programs(1) - 1)
    def _():
        o = acc_ref[...]
        ms = jnp.mean(o * o, axis=-1, keepdims=True)
        o_ref[0] = (o * lax.rsqrt(ms + NORM_EPS) * g_ref[...]).astype(BF16)


def sb_sample(q, k_new, v_new, bias, g_sb, cache_k, cache_v, page_table):
    b, t, heads, dh = q.shape
    tp = SAMPLE_T_PAD
    pps = PAGES_PER_STEP
    pool = cache_k.shape[0]
    n_pages = page_table.shape[1]
    rows = heads * tp
    nh = heads // SUBLANES
    qr = jnp.pad(jnp.swapaxes(q, 1, 2).astype(F32), ((0, 0), (0, 0), (0, tp - t), (0, 0))).reshape(b, rows, dh)
    paged = lambda x: x.reshape(x.shape[0], PAGE_SIZE, nh, SUBLANES, dh)
    as_page = lambda x: paged(jnp.pad(x, ((0, 0), (0, PAGE_SIZE - t), (0, 0), (0, 0))))
    bias_full = jnp.broadcast_to(jnp.repeat(bias, tp)[:, None], (rows, PAGE_SIZE)).astype(F32)
    g_rows = jnp.repeat(g_sb.reshape(heads, dh), tp, axis=0)
    kern = functools.partial(_sb_sample_kernel, heads=heads, pps=pps)
    c2 = lambda i, p, pt: (0, 0)
    per_b = lambda i, p, pt: (i, 0, 0)
    blk = (None, PAGE_SIZE, None, SUBLANES, dh)
    new_specs = [pl.BlockSpec(blk, (lambda hi: lambda i, p, pt: (i, 0, hi, 0, 0))(hi)) for hi in range(nh)]
    page = lambda s, hi: (lambda i, p, pt: (pt[i, n_pages - 1 - (p * pps + s)], 0, hi, 0, 0))
    page_specs = [pl.BlockSpec(blk, page(s, hi)) for s in range(pps) for hi in range(nh)]
    grid_spec = pltpu.PrefetchScalarGridSpec(
        num_scalar_prefetch=1,
        grid=(b, n_pages // pps),
        in_specs=[
            pl.BlockSpec((1, rows, dh), per_b),
            pl.BlockSpec((rows, PAGE_SIZE), c2),
            pl.BlockSpec((rows, dh), c2),
            pl.BlockSpec((PAGE_SIZE, PAGE_SIZE), c2),
            pl.BlockSpec((PAGE_SIZE, PAGE_SIZE), c2),
        ] + new_specs + new_specs + page_specs + page_specs,
        out_specs=pl.BlockSpec((1, rows, dh), per_b),
        scratch_shapes=[
            pltpu.VMEM((rows, PAGE_SIZE), F32),
            pltpu.VMEM((rows, dh), F32),
        ],
    )
    o = pl.pallas_call(
        kern,
        grid_spec=grid_spec,
        out_shape=jax.ShapeDtypeStruct((b, rows, dh), BF16),
        compiler_params=_params("parallel", "arbitrary"),
        name="sb_sample",
    )(page_table, qr, bias_full, g_rows, _later_matrix(PAGE_SIZE), jnp.ones((PAGE_SIZE, PAGE_SIZE), BF16),
      *([as_page(k_new)] * nh), *([as_page(v_new)] * nh),
      *([paged(cache_k)] * (pps * nh)), *([paged(cache_v)] * (pps * nh)))
    return jnp.swapaxes(o.reshape(b, heads, tp, dh)[:, :, :t], 1, 2)


def _rwkv_prep_kernel(*refs, rw, halo_mode, tiles_per_seq):
    if halo_mode:
        u_ref, halo_ref, st_ref = refs[:3]
        rest = refs[3:]
    else:
        u_ref, up_ref = refs[:2]
        rest = refs[2:]
    (mu_ref, w0_ref, w2h_ref, w2l_ref, a0_ref, a2h_ref, a2l_ref, g2h_ref, g2l_ref,
     kkw_ref, ka_ref, rk_ref, bd_ref,
     r_ref, k_ref, v_ref, lw_ref, kk_ref, b_ref, gate_ref, bonus_ref) = rest
    u = u_ref[...]
    if halo_mode:
        tm = u.shape[0]
        row = lax.broadcasted_iota(jnp.int32, (tm, 1), 0)
        first = (pl.program_id(0) % tiles_per_seq) == 0
        prev_row = jnp.where(first, st_ref[0], halo_ref[7:8, :])
        up = jnp.where(row == 0, prev_row, pltpu.roll(u, 1, axis=0))
    else:
        up = up_ref[...]
    us = u + (up - u) * mu_ref[...]
    r = us[:, 0:rw]
    k = us[:, rw:2 * rw]
    v = us[:, 2 * rw:3 * rw]
    o = 3 * rw
    w_lo = us[:, o:o + LANES]
    a_lo = us[:, o + LANES:o + 2 * LANES]
    g_lo = us[:, o + 2 * LANES:o + 4 * LANES]
    d = w0_ref[...] + _dot3(jnp.tanh(w_lo), w2h_ref[...], w2l_ref[...])
    w_log = -_softplus(-d) - 0.5
    lw_ref[...] = -jnp.exp(w_log)
    a = _sigmoid(a0_ref[...] + _dot3(a_lo, a2h_ref[...], a2l_ref[...]))
    gate_ref[...] = _dot3(_sigmoid(g_lo), g2h_ref[...], g2l_ref[...])
    kkr = k * kkw_ref[...]
    nrm = jnp.sqrt(_dot_exact_rhs(kkr * kkr, bd_ref[...]))
    kk = kkr / jnp.maximum(nrm, KK_EPS)
    k_eff = k * (1.0 + (a - 1.0) * ka_ref[...])
    r_ref[...] = r
    k_ref[...] = k_eff
    v_ref[...] = v
    kk_ref[...] = kk
    b_ref[...] = kk * a
    bonus_ref[...] = _dot_exact_rhs(r * k_eff * rk_ref[...], bd_ref[...]) * v


def rwkv_prep(u, prev, pw, rw, seq_len, *, tm, halo_mode):
    n, uw = u.shape
    tm = min(tm, n)
    c = lambda i: (0, 0)
    row_spec = pl.BlockSpec((tm, uw), lambda i: (i, 0))
    if halo_mode:
        tps = seq_len // tm
        halo_blk = tm // 8
        lead_specs = [row_spec,
                      pl.BlockSpec((8, uw), lambda i: (jnp.maximum(i * halo_blk - 1, 0), 0)),
                      pl.BlockSpec((1, 1, uw), lambda i: (i // tps, 0, 0))]
        lead = (u, u, prev)
    else:
        tps = 1
        lead_specs = [row_spec, row_spec]
        lead = (u, prev)
    vec = lambda width: pl.BlockSpec((1, width), c)
    mat = lambda rows: pl.BlockSpec((rows, rw), c)
    in_specs = lead_specs + [
        vec(uw), vec(rw), mat(LANES), mat(LANES), vec(rw), mat(LANES), mat(LANES),
        mat(2 * LANES), mat(2 * LANES), vec(rw), vec(rw), vec(rw), pl.BlockSpec((rw, rw), c)]
    out_spec = pl.BlockSpec((tm, rw), lambda i: (i, 0))
    kern = functools.partial(_rwkv_prep_kernel, rw=rw, halo_mode=halo_mode, tiles_per_seq=tps)
    return pl.pallas_call(
        kern,
        grid=(n // tm,),
        in_specs=in_specs,
        out_specs=[out_spec] * 8,
        out_shape=[jax.ShapeDtypeStruct((n, rw), F32)] * 8,
        compiler_params=_params("parallel"),
        name="rwkv_prep",
    )(*lead, pw['mu'], pw['w0'], pw['w2h'], pw['w2l'], pw['a0'], pw['a2h'], pw['a2l'],
      pw['g2h'], pw['g2l'], pw['kk'], pw['ka'], pw['rk'], pw['bd'])


def _rwkv_chunk_kernel(r_ref, k_ref, v_ref, lw_ref, kk_ref, b_ref, y_ref, s_out_ref, s_ref, *, chunk, npairs):
    c = pl.program_id(1)
    C = chunk
    C2 = HEADS_PER_TILE * C

    @pl.when(c == 0)
    def _():
        s_ref[...] = jnp.zeros_like(s_ref)

    ci = lax.broadcasted_iota(jnp.int32, (C, C), 0)
    cj = lax.broadcasted_iota(jnp.int32, (C, C), 1)
    ti = lax.broadcasted_iota(jnp.int32, (C2, C2), 0)
    tj = lax.broadcasted_iota(jnp.int32, (C2, C2), 1)
    same_blk = (ti // C) == (tj // C)
    incl = same_blk & (ti >= tj)
    strict = same_blk & (ti > tj)
    eye = (ti == tj).astype(F32)
    lane_head = lax.broadcasted_iota(jnp.int32, (C2, LANES), 1) // HEAD_DIM
    row_head = lax.broadcasted_iota(jnp.int32, (C2, LANES), 0) // C
    own_lanes = lane_head == row_head

    def stack(x):
        return jnp.where(own_lanes, jnp.concatenate([x] * HEADS_PER_TILE, axis=0), jnp.zeros((), BF16))

    lw = lw_ref[...]
    l_hi, l_mid, l_lo = _split3(lw)
    tri = (ci >= cj).astype(BF16)
    cl = _dot(tri, l_hi) + (_dot(tri, l_mid) + _dot(tri, l_lo))
    cl_end = cl[C - 1:C, :]
    e_pos = jnp.exp(cl)
    e_neg = jnp.exp(-cl)
    e_prev = jnp.exp(cl - lw)
    e_end = jnp.exp(cl_end - cl)
    all_ones = jnp.ones((C, LANES), BF16)
    d_rows = jnp.exp(_dot_tn(l_hi, all_ones) + (_dot_tn(l_mid, all_ones) + _dot_tn(l_lo, all_ones)))
    r = r_ref[...]
    k = k_ref[...]
    b = b_ref[...]
    rt = (r * e_pos).astype(BF16)
    kt = (kk_ref[...] * e_prev).astype(BF16)
    k_inv = (k * e_neg).astype(BF16)
    b_inv = (b * e_neg).astype(BF16)
    k_dec = (k * e_end).astype(BF16)
    b_dec = (b * e_end).astype(BF16)
    vb = v_ref[...].astype(BF16)

    tiles = range(npairs)
    sls = [slice(p * LANES, (p + 1) * LANES) for p in tiles]
    st = [s_ref[p] for p in tiles]
    stb = [s.astype(BF16) for s in st]
    v2 = [stack(vb[:, sl]) for sl in sls]
    kt2 = [stack(kt[:, sl]) for sl in sls]
    rt2 = [stack(rt[:, sl]) for sl in sls]
    ab = [_dot_nt(jnp.concatenate([kt2[p], rt2[p]], axis=0),
                  jnp.concatenate([stack(k_inv[:, sls[p]]), stack(b_inv[:, sls[p]])], axis=0))
          for p in tiles]
    a_m = [jnp.where(strict, m[:C2, :C2], 0.0).astype(BF16) for m in ab]
    n_m = [jnp.where(strict, -m[:C2, C2:], 0.0) for m in ab]
    aq_m = [jnp.where(incl, m[C2:, :C2], 0.0).astype(BF16) for m in ab]
    nbq_m = [jnp.where(incl, -m[C2:, C2:], 0.0).astype(BF16) for m in ab]
    g = [eye + m for m in n_m]
    pw = [_dot(m.astype(BF16), m.astype(BF16)) for m in n_m]
    levels = int(math.log2(C))
    for lvl in range(1, levels):
        if lvl < levels - 1:
            gp = [_dot(jnp.concatenate([g[p], pw[p]], axis=0).astype(BF16), pw[p].astype(BF16)) for p in tiles]
            g = [g[p] + gp[p][:C2] for p in tiles]
            pw = [m[C2:] for m in gp]
        else:
            g = [g[p] + _dot(g[p].astype(BF16), pw[p].astype(BF16)) for p in tiles]
    x = [_dot(jnp.concatenate([a_m[p], kt2[p]], axis=1), jnp.concatenate([v2[p], stb[p]], axis=0)) for p in tiles]
    u2b = [_dot(g[p].astype(BF16), x[p].astype(BF16)).astype(BF16) for p in tiles]
    y2 = [_dot(jnp.concatenate([aq_m[p], nbq_m[p], rt2[p]], axis=1),
               jnp.concatenate([v2[p], u2b[p], stb[p]], axis=0)) for p in tiles]
    upd = [_dot_tn(jnp.concatenate([stack(k_dec[:, sls[p]]), stack(b_dec[:, sls[p]])], axis=0),
                   jnp.concatenate([v2[p], -u2b[p]], axis=0)) for p in tiles]
    for p in tiles:
        y = y2[p][:C]
        for hh in range(1, HEADS_PER_TILE):
            y = y + y2[p][hh * C:(hh + 1) * C]
        y_ref[:, sls[p]] = y
        s_new = st[p] * d_rows[sls[p], :] + upd[p]
        s_ref[p] = s_new
        s_out_ref[0, p] = s_new


def rwkv_chunk(r, k, v, lw, kk, b, batch, seq_len):
    n, rw = r.shape
    npairs = rw // LANES
    C = RW_CHUNK
    nc = seq_len // C
    blk = pl.BlockSpec((C, rw), lambda i, c: (i * nc + c, 0))
    kern = functools.partial(_rwkv_chunk_kernel, chunk=C, npairs=npairs)
    return pl.pallas_call(
        kern,
        grid=(batch, nc),
        in_specs=[blk] * 6,
        out_specs=[blk, pl.BlockSpec((1, npairs, LANES, LANES), lambda i, c: (i, 0, 0, 0))],
        out_shape=[jax.ShapeDtypeStruct((n, rw), F32),
                   jax.ShapeDtypeStruct((batch, npairs, LANES, LANES), F32)],
        scratch_shapes=[pltpu.VMEM((npairs, LANES, LANES), F32)],
        compiler_params=_params("parallel", "arbitrary"),
        name="rwkv_chunk",
    )(r, k, v, lw, kk, b)


def _rwkv_seq_kernel(s0_ref, r_ref, k_ref, lw_ref, kk_ref, b_ref, vcol_ref, ycol_ref, s_out_ref, *, steps):
    s = s0_ref[0]
    for t in range(steps):
        sa = jnp.sum(s * kk_ref[0, t], axis=-1, keepdims=True)
        s = s * jnp.exp(lw_ref[0, t]) - sa * b_ref[0, t] + vcol_ref[0, t] * k_ref[0, t]
        ycol_ref[0, t] = jnp.sum(s * r_ref[0, t], axis=-1, keepdims=True)
    s_out_ref[0] = s


def rwkv_seq(s0, r, k, v, lw, kk, b):
    bsz, heads, dh, _ = s0.shape
    t = r.shape[1]
    rowv = lambda x: x.reshape(bsz, t, heads, 1, dh)
    row_spec = pl.BlockSpec((1, t, heads, 1, dh), lambda i: (i, 0, 0, 0, 0))
    col_spec = pl.BlockSpec((1, t, heads, dh, 1), lambda i: (i, 0, 0, 0, 0))
    st_spec = pl.BlockSpec((1, heads, dh, dh), lambda i: (i, 0, 0, 0))
    kern = functools.partial(_rwkv_seq_kernel, steps=t)
    ycol, s = pl.pallas_call(
        kern,
        grid=(bsz,),
        in_specs=[st_spec] + [row_spec] * 5 + [col_spec],
        out_specs=[col_spec, st_spec],
        out_shape=[jax.ShapeDtypeStruct((bsz, t, heads, dh, 1), F32),
                   jax.ShapeDtypeStruct(s0.shape, F32)],
        compiler_params=_params("parallel"),
        name="rwkv_seq",
    )(s0, rowv(r), rowv(k), rowv(lw), rowv(kk), rowv(b), v.reshape(bsz, t, heads, dh, 1))
    return ycol.reshape(bsz, t, heads * dh), s


def _rwkv_epi_kernel(y_ref, bonus_ref, gate_ref, g_ref, b_ref, bd_ref, o_ref):
    y = y_ref[...]
    inv = 1.0 / HEAD_DIM
    mu = _dot_exact_rhs(y, bd_ref[...]) * inv
    yc = y - mu
    var = _dot_exact_rhs(yc * yc, bd_ref[...]) * inv
    yn = yc * lax.rsqrt(var + LNX_EPS) * g_ref[...] + b_ref[...]
    o_ref[...] = ((yn + bonus_ref[...]) * gate_ref[...]).astype(BF16)


def rwkv_epilogue(y, bonus, gate, ln_g, ln_b, bd, *, tm):
    n, rw = y.shape
    tm = min(tm, n)
    blk = pl.BlockSpec((tm, rw), lambda i: (i, 0))
    vec = pl.BlockSpec((1, rw), lambda i: (0, 0))
    return pl.pallas_call(
        _rwkv_epi_kernel,
        grid=(n // tm,),
        in_specs=[blk, blk, blk, vec, vec, pl.BlockSpec((rw, rw), lambda i: (0, 0))],
        out_specs=blk,
        out_shape=jax.ShapeDtypeStruct((n, rw), BF16),
        compiler_params=_params("parallel"),
        name="rwkv_epilogue",
    )(y, bonus, gate, ln_g, ln_b, bd)


def _pad_rows(m, rows):
    return jnp.pad(m, ((0, rows - m.shape[0]), (0, 0)))


def _hi_lo(m):
    hi = m.astype(BF16)
    return hi, (m - hi.astype(F32)).astype(BF16)


def _u_layout(rw, n_decay, n_icl, n_gate):
    segs = [(3 * rw, 3 * rw), (n_decay, LANES), (n_icl, LANES), (n_gate, 2 * LANES)]
    assert n_decay <= LANES and n_icl <= LANES and n_gate <= 2 * LANES
    return segs


def _pad_u_cols(m, segs):
    out, o = [], 0
    for width, padded in segs:
        out.append(m[..., o:o + width])
        if padded > width:
            out.append(jnp.zeros(m.shape[:-1] + (padded - width,), m.dtype))
        o += width
    return jnp.concatenate(out, axis=-1)


def _unpad_u_cols(m, segs):
    out, o = [], 0
    for width, padded in segs:
        out.append(m[..., o:o + width])
        o += padded
    return jnp.concatenate(out, axis=-1)


def _prepare_weights(w_in, w_out, g_pre_mix, g_post_mix, g_pre_ffn, g_post_ffn, g_sb_out, sb_bias,
                     rw_mu, rw_w0, rw_w2, rw_a0, rw_a2, rw_g2, rw_kk, rw_ka, rw_rk, rw_ln_g, rw_ln_b,
                     w_up, conv_w, conv_b, w_down):
    rw = rw_w0.shape[0]
    sbw = g_sb_out.shape[0]
    segs = _u_layout(rw, rw_w2.shape[0], rw_a2.shape[0], rw_g2.shape[0])
    row = lambda x: x.reshape(1, -1)
    w2h, w2l = _hi_lo(_pad_rows(rw_w2, LANES))
    a2h, a2l = _hi_lo(_pad_rows(rw_a2, LANES))
    g2h, g2l = _hi_lo(_pad_rows(rw_g2, 2 * LANES))
    bd = _head_block_ones(rw)
    w_in_p = jnp.concatenate([w_in[:, :3 * sbw], _pad_u_cols(w_in[:, 3 * sbw:], segs)], axis=1).astype(BF16)
    prep = dict(mu=row(_pad_u_cols(rw_mu, segs)), w0=row(rw_w0), w2h=w2h, w2l=w2l, a0=row(rw_a0),
                a2h=a2h, a2l=a2l, g2h=g2h, g2l=g2l, kk=row(rw_kk), ka=row(rw_ka),
                rk=row(rw_rk), bd=bd)
    return dict(
        segs=segs, rw=rw, sbw=sbw, w_in=w_in_p, w_out=w_out.astype(BF16),
        g_pre_mix=row(g_pre_mix), g_post_mix=row(g_post_mix), g_pre_ffn=row(g_pre_ffn),
        g_post_ffn=row(g_post_ffn), g_sb=row(g_sb_out), sb_bias=sb_bias, prep=prep,
        ln_g=row(rw_ln_g), ln_b=row(rw_ln_b), bd=bd,
        w_up=w_up.astype(BF16), conv_w=conv_w, conv_b=row(conv_b), w_down=w_down.astype(BF16))


def _tail(wts, x, o_sb, o_rw, *, tm):
    mixed_in = jnp.concatenate([o_sb, o_rw], axis=-1)
    return mm_norm_resid(mixed_in, wts['w_out'], x, wts['g_post_mix'], wts['g_pre_ffn'],
                         tm=tm, tk=mixed_in.shape[1], emit_h=True)


def _down(wts, act, x1, *, tm):
    (y,) = mm_norm_resid(act, wts['w_down'], x1, wts['g_post_ffn'], wts['g_post_ffn'],
                         tm=tm, tk=act.shape[1] // 4, emit_h=False)
    return y


def _prompt_layer(wts, x3):
    bsz, t, d = x3.shape
    n = bsz * t
    x = x3.reshape(n, d)
    rw, sbw, segs = wts['rw'], wts['sbw'], wts['segs']
    uw = sum(p for _, p in segs)
    q, k, v, u = in_proj(x, wts['g_pre_mix'], wts['w_in'], sbw, uw, tm=512)
    o_sb = sb_prompt(q, k, v, wts['sb_bias'], wts['g_sb'], bsz, t, tq=min(512, t), tk=min(256, t))
    shift0 = jnp.zeros((bsz, 1, uw), F32)
    r, ke, vr, lw, kk, b, gate, bonus = rwkv_prep(u, shift0, wts['prep'], rw, t, tm=min(256, t), halo_mode=True)
    y, s_pairs = rwkv_chunk(r, ke, vr, lw, kk, b, bsz, t)
    o_rw = rwkv_epilogue(y, bonus, gate, wts['ln_g'], wts['ln_b'], wts['bd'], tm=512)
    x1, h2 = _tail(wts, x, o_sb, o_rw, tm=512)
    tm_f = min(512, t)
    conv0 = jnp.zeros((bsz, CONV_W - 1, wts['conv_w'].shape[1]), F32)
    act, gtail = ffn_up_prompt(h2, wts['w_up'], wts['conv_w'], wts['conv_b'], conv0, t, tm=tm_f)
    yout = _down(wts, act, x1, tm=512)
    heads = rw // HEAD_DIM
    hp = HEADS_PER_TILE
    sp = s_pairs.reshape(bsz, rw // LANES, hp, HEAD_DIM, hp, HEAD_DIM)
    wkv = jnp.stack([jnp.swapaxes(sp[:, :, i, :, i, :], -1, -2) for i in range(hp)], axis=2)
    wkv = wkv.reshape(bsz, heads, HEAD_DIM, HEAD_DIM)
    shift = _unpad_u_cols(u.reshape(bsz, t, uw)[:, -1], segs)
    tps = t // tm_f
    conv_new = gtail.reshape(bsz, tps, 8, -1)[:, -1, 8 - (CONV_W - 1):]
    kvshape = (bsz, t, sbw // HEAD_DIM, HEAD_DIM)
    return yout.reshape(bsz, t, d), k.reshape(kvshape), v.reshape(kvshape), wkv, shift, conv_new


def _sample_layer(wts, x3, cache_k, cache_v, page_table, wkv0, shift_prev, conv_prev):
    bsz, t, d = x3.shape
    n = bsz * t
    x = x3.reshape(n, d)
    rw, sbw, segs = wts['rw'], wts['sbw'], wts['segs']
    uw = sum(p for _, p in segs)
    q, k, v, u = in_proj(x, wts['g_pre_mix'], wts['w_in'], sbw, uw, tm=n)
    kvshape = (bsz, t, sbw // HEAD_DIM, HEAD_DIM)
    o_sb = sb_sample(q.reshape(kvshape), k.reshape(kvshape), v.reshape(kvshape),
                     wts['sb_bias'], wts['g_sb'], cache_k, cache_v, page_table).reshape(n, sbw)
    u3 = u.reshape(bsz, t, uw)
    u_prev = jnp.concatenate([_pad_u_cols(shift_prev, segs)[:, None], u3[:, :-1]], axis=1).reshape(n, uw)
    r, ke, vr, lw, kk, b, gate, bonus = rwkv_prep(u, u_prev, wts['prep'], rw, t, tm=n, halo_mode=False)
    as3 = lambda a: a.reshape(bsz, t, rw)
    y3, wkv = rwkv_seq(wkv0, as3(r), as3(ke), as3(vr), as3(lw), as3(kk), as3(b))
    o_rw = rwkv_epilogue(y3.reshape(n, rw), bonus, gate, wts['ln_g'], wts['ln_b'], wts['bd'], tm=n)
    x1, h2 = _tail(wts, x, o_sb, o_rw, tm=n)
    f = wts['conv_w'].shape[1]
    zeros = jnp.zeros((bsz, t, f), F32)
    p1 = zeros.at[:, 0].set(conv_prev[:, 1]).reshape(n, f)
    p2 = zeros.at[:, 0].set(conv_prev[:, 0]).at[:, 1].set(conv_prev[:, 1]).reshape(n, f)
    act, gfull = ffn_up_sample(h2, wts['w_up'], wts['conv_w'], wts['conv_b'], p1, p2, t)
    yout = _down(wts, act, x1, tm=n)
    shift = _unpad_u_cols(u3[:, -1], segs)
    conv_new = gfull.reshape(bsz, t, f)[:, t - (CONV_W - 1):]
    kvshape = (bsz, t, sbw // HEAD_DIM, HEAD_DIM)
    return yout.reshape(bsz, t, d), k.reshape(kvshape), v.reshape(kvshape), wkv, shift, conv_new


def kernel(x_prompt, x_sample, cache_sb_k, cache_sb_v, state_rwkv_wkv, state_rwkv_shift, state_ffn_conv, page_table, w_in, w_out, g_pre_mix, g_post_mix, g_pre_ffn, g_post_ffn, g_sb_out, sb_bias, rw_mu, rw_w0, rw_w2, rw_a0, rw_a2, rw_g2, rw_kk, rw_ka, rw_rk, rw_ln_g, rw_ln_b, w_up, conv_w, conv_b, w_down):
    depth = w_in.shape[0]
    yp, ys = x_prompt, x_sample
    outs_p, outs_s = [], []
    for l in range(depth):
        wts = _prepare_weights(w_in[l], w_out[l], g_pre_mix[l], g_post_mix[l], g_pre_ffn[l], g_post_ffn[l],
                               g_sb_out[l], sb_bias[l], rw_mu[l], rw_w0[l], rw_w2[l], rw_a0[l], rw_a2[l],
                               rw_g2[l], rw_kk[l], rw_ka[l], rw_rk[l].reshape(-1), rw_ln_g[l], rw_ln_b[l],
                               w_up[l], conv_w[l], conv_b[l], w_down[l])
        yp, *rest_p = _prompt_layer(wts, yp)
        ys, *rest_s = _sample_layer(wts, ys, cache_sb_k[l], cache_sb_v[l], page_table,
                                    state_rwkv_wkv[l], state_rwkv_shift[l], state_ffn_conv[l])
        outs_p.append(rest_p)
        outs_s.append(rest_s)
    stack = lambda outs, i: jnp.stack([o[i] for o in outs])
    return (yp, ys,
            stack(outs_p, 0), stack(outs_p, 1), stack(outs_p, 2), stack(outs_p, 3), stack(outs_p, 4),
            stack(outs_s, 0), stack(outs_s, 1), stack(outs_s, 2), stack(outs_s, 3), stack(outs_s, 4))
```

```python
import functools
import math

import jax
import jax.numpy as jnp
from jax import lax
from jax.experimental import pallas as pl
from jax.experimental.pallas import tpu as pltpu

F32 = jnp.float32
BF16 = jnp.bfloat16

HEAD_DIM = 64
LANES = 128
HEADS_PER_TILE = LANES // HEAD_DIM
NORM_EPS = 1e-6
LNX_EPS = 64e-5
KK_EPS = 1e-12
CONV_W = 3
PAGE_SIZE = 128
VMEM_LIMIT = 48 * 1024 * 1024
RW_CHUNK = 64


def _dot(a, b):
    return jnp.dot(a, b, preferred_element_type=F32)


def _dot_nt(a, b):
    return lax.dot_general(a, b, (((1,), (1,)), ((), ())), preferred_element_type=F32)


def _dot_tn(a, b):
    return lax.dot_general(a, b, (((0,), (0,)), ((), ())), preferred_element_type=F32)


def _split2(x):
    hi = x.astype(BF16)
    lo = (x - hi.astype(F32)).astype(BF16)
    return hi, lo


def _split3(x):
    hi = x.astype(BF16)
    r1 = x - hi.astype(F32)
    mid = r1.astype(BF16)
    lo = (r1 - mid.astype(F32)).astype(BF16)
    return hi, mid, lo


def _dot_exact_rhs(x, m01):
    hi, lo = _split2(x)
    return _dot(hi, m01) + _dot(lo, m01)


def _dot3(a, b_hi, b_lo):
    a_hi, a_lo = _split2(a)
    return _dot(a_hi, b_hi) + (_dot(a_lo, b_hi) + _dot(a_hi, b_lo))


def _softplus(x):
    return jnp.maximum(x, 0.0) + jnp.log(1.0 + jnp.exp(-jnp.abs(x)))


def _sigmoid(x):
    return 1.0 / (1.0 + jnp.exp(-x))


def _rms_rows(x, g):
    return x * lax.rsqrt(jnp.mean(x * x, axis=-1, keepdims=True) + NORM_EPS) * g


def _params(*sem):
    return pltpu.CompilerParams(dimension_semantics=sem, vmem_limit_bytes=VMEM_LIMIT)


def _head_block_ones(n):
    i = jnp.arange(n) // HEAD_DIM
    return (i[:, None] == i[None, :]).astype(BF16)


def _in_proj_kernel(x_ref, g_ref, w_ref, q_ref, k_ref, v_ref, u_ref, h_ref, *, nq, nk, nv):
    j = pl.program_id(1)

    @pl.when(j == 0)
    def _():
        h_ref[...] = _rms_rows(x_ref[...], g_ref[...]).astype(BF16)

    res = _dot(h_ref[...], w_ref[...])

    @pl.when(j < nq)
    def _():
        q_ref[...] = (res * (HEAD_DIM ** -0.5)).astype(BF16)

    @pl.when((j >= nq) & (j < nq + nk))
    def _():
        k_ref[...] = res

    @pl.when((j >= nq + nk) & (j < nq + nk + nv))
    def _():
        v_ref[...] = res

    @pl.when(j >= nq + nk + nv)
    def _():
        u_ref[...] = res


def in_proj(x, g, w, sb_width, u_width, *, tm, tn=512):
    n, d = x.shape
    tm = min(tm, n)
    nq = nk = nv = sb_width // tn
    nu = u_width // tn
    nj = nq + nk + nv + nu
    off_k, off_v, off_u = nq, nq + nk, nq + nk + nv
    clip = lambda j, lo, cnt: jnp.clip(j - lo, 0, cnt - 1)
    kern = functools.partial(_in_proj_kernel, nq=nq, nk=nk, nv=nv)
    return pl.pallas_call(
        kern,
        grid=(n // tm, nj),
        in_specs=[
            pl.BlockSpec((tm, d), lambda i, j: (i, 0)),
            pl.BlockSpec((1, d), lambda i, j: (0, 0)),
            pl.BlockSpec((d, tn), lambda i, j: (0, j)),
        ],
        out_specs=[
            pl.BlockSpec((tm, tn), lambda i, j: (i, clip(j, 0, nq))),
            pl.BlockSpec((tm, tn), lambda i, j: (i, clip(j, off_k, nk))),
            pl.BlockSpec((tm, tn), lambda i, j: (i, clip(j, off_v, nv))),
            pl.BlockSpec((tm, tn), lambda i, j: (i, clip(j, off_u, nu))),
        ],
        out_shape=[
            jax.ShapeDtypeStruct((n, sb_width), BF16),
            jax.ShapeDtypeStruct((n, sb_width), F32),
            jax.ShapeDtypeStruct((n, sb_width), F32),
            jax.ShapeDtypeStruct((n, u_width), F32),
        ],
        scratch_shapes=[pltpu.VMEM((tm, d), BF16)],
        compiler_params=_params("parallel", "arbitrary"),
        name="in_proj",
    )(x, g, w)


def _mm_norm_resid_kernel(a_ref, w_ref, r_ref, g_ref, g2_ref, *rest, nk, emit_h):
    if emit_h:
        o_ref, h_ref, acc_ref = rest
    else:
        o_ref, acc_ref = rest
        h_ref = None
    kk = pl.program_id(1)
    part = _dot(a_ref[...], w_ref[...])

    def finalize(m):
        y = r_ref[...] + _rms_rows(m, g_ref[...])
        o_ref[...] = y
        if emit_h:
            h_ref[...] = _rms_rows(y, g2_ref[...]).astype(BF16)

    if nk == 1:
        finalize(part)
    else:
        @pl.when(kk == 0)
        def _():
            acc_ref[...] = part

        @pl.when((kk > 0) & (kk < nk - 1))
        def _():
            acc_ref[...] += part

        @pl.when(kk == nk - 1)
        def _():
            finalize(acc_ref[...] + part)


def mm_norm_resid(a, w, resid, g, g2, *, tm, tk, emit_h):
    n, kdim = a.shape
    d = w.shape[1]
    tm = min(tm, n)
    nk = kdim // tk
    kern = functools.partial(_mm_norm_resid_kernel, nk=nk, emit_h=emit_h)
    out_specs = [pl.BlockSpec((tm, d), lambda i, k: (i, 0))]
    out_shape = [jax.ShapeDtypeStruct((n, d), F32)]
    if emit_h:
        out_specs.append(pl.BlockSpec((tm, d), lambda i, k: (i, 0)))
        out_shape.append(jax.ShapeDtypeStruct((n, d), BF16))
    return pl.pallas_call(
        kern,
        grid=(n // tm, nk),
        in_specs=[
            pl.BlockSpec((tm, tk), lambda i, k: (i, k)),
            pl.BlockSpec((tk, d), lambda i, k: (k, 0)),
            pl.BlockSpec((tm, d), lambda i, k: (i, 0)),
            pl.BlockSpec((1, d), lambda i, k: (0, 0)),
            pl.BlockSpec((1, d), lambda i, k: (0, 0)),
        ],
        out_specs=out_specs,
        out_shape=out_shape,
        scratch_shapes=[pltpu.VMEM((tm, d), F32)],
        compiler_params=_params("parallel", "arbitrary"),
        name="mm_norm_resid",
    )(a, w, resid, g, g2)


def _gelu_tanh(x):
    c = math.sqrt(2.0 / math.pi)
    return x * (0.5 * (1.0 + jnp.tanh(c * (x + 0.044715 * (x * x * x)))))


def _ffn_up_kernel(*refs, tiles_per_seq, seq_len, halo_mode):
    if halo_mode:
        h_ref, halo_ref, wg_ref, wv_ref, cw_ref, cb_ref, prev_ref, act_ref, g_out_ref = refs
    else:
        h_ref, wg_ref, wv_ref, cw_ref, cb_ref, p1_ref, p2_ref, act_ref, g_out_ref = refs
    h = h_ref[...]
    g = _dot(h, wg_ref[...])
    val = _dot(h, wv_ref[...])
    tm = g.shape[0]
    row = lax.broadcasted_iota(jnp.int32, (tm, 1), 0)
    r1 = pltpu.roll(g, 1, axis=0)
    r2 = pltpu.roll(g, 2, axis=0)
    if halo_mode:
        gh = _dot(halo_ref[...], wg_ref[...])
        first = (pl.program_id(0) % tiles_per_seq) == 0
        pm1 = jnp.where(first, prev_ref[0, 1:2, :], gh[7:8, :])
        pm2 = jnp.where(first, prev_ref[0, 0:1, :], gh[6:7, :])
        g1 = jnp.where(row == 0, pm1, r1)
        g2 = jnp.where(row == 0, pm2, jnp.where(row == 1, pm1, r2))
        g_out_ref[0] = g[tm - 8:, :]
    else:
        t = row % seq_len
        g1 = jnp.where(t >= 1, r1, p1_ref[...])
        g2 = jnp.where(t >= 2, r2, p2_ref[...])
        g_out_ref[...] = g
    conv = cb_ref[...] + cw_ref[0:1, :] * g2 + cw_ref[1:2, :] * g1 + cw_ref[2:3, :] * g
    act_ref[...] = (_gelu_tanh(conv) * val).astype(BF16)


def ffn_up_prompt(h, w_up, conv_w, conv_b, conv_prev, seq_len, *, tm, tn=512):
    n, d = h.shape
    f = w_up.shape[1] // 2
    nf = f // tn
    tps = seq_len // tm
    kern = functools.partial(_ffn_up_kernel, tiles_per_seq=tps, seq_len=seq_len, halo_mode=True)
    halo_blk = tm // 8
    return pl.pallas_call(
        kern,
        grid=(n // tm, nf),
        in_specs=[
            pl.BlockSpec((tm, d), lambda i, j: (i, 0)),
            pl.BlockSpec((8, d), lambda i, j: (jnp.maximum(i * halo_blk - 1, 0), 0)),
            pl.BlockSpec((d, tn), lambda i, j: (0, j)),
            pl.BlockSpec((d, tn), lambda i, j: (0, j + nf)),
            pl.BlockSpec((CONV_W, tn), lambda i, j: (0, j)),
            pl.BlockSpec((1, tn), lambda i, j: (0, j)),
            pl.BlockSpec((1, CONV_W - 1, tn), lambda i, j: (i // tps, 0, j)),
        ],
        out_specs=[
            pl.BlockSpec((tm, tn), lambda i, j: (i, j)),
            pl.BlockSpec((1, 8, tn), lambda i, j: (i, 0, j)),
        ],
        out_shape=[
            jax.ShapeDtypeStruct((n, f), BF16),
            jax.ShapeDtypeStruct((n // tm, 8, f), F32),
        ],
        compiler_params=_params("parallel", "arbitrary"),
        name="ffn_up_prompt",
    )(h, h, w_up, w_up, conv_w, conv_b, conv_prev)


def ffn_up_sample(h, w_up, conv_w, conv_b, p1, p2, seq_len, *, tn=512):
    n, d = h.shape
    f = w_up.shape[1] // 2
    nf = f // tn
    kern = functools.partial(_ffn_up_kernel, tiles_per_seq=1, seq_len=seq_len, halo_mode=False)
    return pl.pallas_call(
        kern,
        grid=(1, nf),
        in_specs=[
            pl.BlockSpec((n, d), lambda i, j: (0, 0)),
            pl.BlockSpec((d, tn), lambda i, j: (0, j)),
            pl.BlockSpec((d, tn), lambda i, j: (0, j + nf)),
            pl.BlockSpec((CONV_W, tn), lambda i, j: (0, j)),
            pl.BlockSpec((1, tn), lambda i, j: (0, j)),
            pl.BlockSpec((n, tn), lambda i, j: (0, j)),
            pl.BlockSpec((n, tn), lambda i, j: (0, j)),
        ],
        out_specs=[
            pl.BlockSpec((n, tn), lambda i, j: (0, j)),
            pl.BlockSpec((n, tn), lambda i, j: (0, j)),
        ],
        out_shape=[
            jax.ShapeDtypeStruct((n, f), BF16),
            jax.ShapeDtypeStruct((n, f), F32),
        ],
        compiler_params=_params("arbitrary", "arbitrary"),
        name="ffn_up_sample",
    )(h, w_up, w_up, conv_w, conv_b, p1, p2)


def _sb_prompt_kernel(bias_ref, q_ref, k_ref, v_ref, later_ref, bd_ref, g_ref, o_ref,
                      kb_ref, vb_ref, carry_ref, acc_ref, *, tq, tk):
    p = pl.program_id(1)
    qi = pl.program_id(2)
    ratio = tq // tk

    @pl.when(qi == 0)
    def _():
        kb_ref[...] = k_ref[...].astype(BF16)
        vb_ref[...] = v_ref[...].astype(BF16)

    q = q_ref[...]
    lane = lax.broadcasted_iota(jnp.int32, (1, LANES), 1)
    qms = [jnp.where((lane // HEAD_DIM) == hh, q, jnp.zeros_like(q)) for hh in range(HEADS_PER_TILE)]
    biases = [bias_ref[p * HEADS_PER_TILE + hh] for hh in range(HEADS_PER_TILE)]
    carry_ref[...] = jnp.zeros_like(carry_ref)
    acc_ref[...] = jnp.zeros_like(acc_ref)
    later = later_ref[...]

    def key_tile(j, masked):
        j0 = pl.multiple_of(j * tk, tk)
        kblk = kb_ref[pl.ds(j0, tk), :]
        vblk = vb_ref[pl.ds(j0, tk), :]
        mask = None
        if masked:
            qpos = qi * tq + lax.broadcasted_iota(jnp.int32, (tq, tk), 0)
            kpos = j0 + lax.broadcasted_iota(jnp.int32, (tq, tk), 1)
            mask = kpos < qpos
        hs = range(HEADS_PER_TILE)
        z = [_dot_nt(qms[hh], kblk) + biases[hh] for hh in hs]
        c = [carry_ref[hh] for hh in hs]
        sp = [_softplus(z[hh]) for hh in hs]
        if mask is not None:
            sp = [jnp.where(mask, s, 0.0) for s in sp]
        ls = [z[hh] - sp[hh] for hh in hs]
        after = [_dot(s.astype(BF16), later) for s in sp]
        w = [jnp.exp(ls[hh] - after[hh] - jnp.concatenate([c[hh]] * (tk // LANES), axis=1)) for hh in hs]
        if mask is not None:
            w = [jnp.where(mask, x, 0.0) for x in w]
        pv = [_dot(x.astype(BF16), vblk) for x in w]
        for hh in hs:
            acc_ref[hh] += pv[hh]
            carry_ref[hh] = c[hh] + jnp.sum(sp[hh], axis=-1, keepdims=True)

    for d in range(ratio - 1, -1, -1):
        key_tile(qi * ratio + d, True)

    def body(it, c):
        key_tile(qi * ratio - 1 - it, False)
        return c

    lax.fori_loop(0, qi * ratio, body, 0)
    o = acc_ref[0]
    for hh in range(1, HEADS_PER_TILE):
        o = jnp.where((lane // HEAD_DIM) == hh, acc_ref[hh], o)
    ms = _dot_exact_rhs(o * o, bd_ref[...]) * (1.0 / HEAD_DIM)
    o_ref[...] = (o * lax.rsqrt(ms + NORM_EPS) * g_ref[...]).astype(BF16)


def _later_matrix(n):
    idx = jnp.arange(n)
    return (idx[:, None] > idx[None, :]).astype(BF16)


def sb_prompt(q, k, v, bias, g_sb, batch, seq_len, *, tq, tk):
    n, w = q.shape
    npairs = w // LANES
    nq = seq_len // tq
    kern = functools.partial(_sb_prompt_kernel, tq=tq, tk=tk)
    const = lambda b, p, i: (0, 0)
    return pl.pallas_call(
        kern,
        grid=(batch, npairs, nq),
        in_specs=[
            pl.BlockSpec(memory_space=pltpu.SMEM),
            pl.BlockSpec((tq, LANES), lambda b, p, i: (b * nq + i, p)),
            pl.BlockSpec((seq_len, LANES), lambda b, p, i: (b, p)),
            pl.BlockSpec((seq_len, LANES), lambda b, p, i: (b, p)),
            pl.BlockSpec((tk, tk), const),
            pl.BlockSpec((LANES, LANES), const),
            pl.BlockSpec((1, LANES), lambda b, p, i: (0, p)),
        ],
        out_specs=pl.BlockSpec((tq, LANES), lambda b, p, i: (b * nq + i, p)),
        out_shape=jax.ShapeDtypeStruct((n, w), BF16),
        scratch_shapes=[
            pltpu.VMEM((seq_len, LANES), BF16),
            pltpu.VMEM((seq_len, LANES), BF16),
            pltpu.VMEM((HEADS_PER_TILE, tq, LANES), F32),
            pltpu.VMEM((HEADS_PER_TILE, tq, LANES), F32),
        ],
        compiler_params=_params("parallel", "parallel", "arbitrary"),
        name="sb_prompt",
    )(bias, q, k, v, _later_matrix(tk), _head_block_ones(LANES), g_sb)


SAMPLE_T_PAD = 8
PAGES_PER_STEP = 4


def _sb_sample_kernel(pt_ref, q_ref, bias_ref, g_ref, later_ref, ones_ref, hm_ref, bd_ref, *rest, heads, pps):
    del pt_ref
    kn, vn = rest[0:1], rest[1:2]
    kp, vp = rest[2:2 + pps], rest[2 + pps:2 + 2 * pps]
    o_ref, carry_ref, acc_ref = rest[2 + 2 * pps:]
    pg = pl.program_id(1)
    qx = q_ref[0]
    rows = qx.shape[0]
    bias = bias_ref[...]
    later = later_ref[...]
    ones = ones_ref[...]

    def sweep(k_refs, v_refs, carry, acc, mask):
        n = len(k_refs)
        z = [_dot(qx, k_refs[i][...].astype(BF16)) + bias for i in range(n)]
        sp = [_softplus(x) for x in z]
        if mask is not None:
            sp = [jnp.where(mask, s, 0.0) for s in sp]
        spb = [s.astype(BF16) for s in sp]
        after = [_dot(s, later) for s in spb]
        total = [_dot(s, ones) for s in spb]
        w = []
        for i in range(n):
            wi = jnp.exp(z[i] - sp[i] - after[i] - carry)
            w.append(wi if mask is None else jnp.where(mask, wi, 0.0))
            carry = carry + total[i]
        pv = [_dot_nt(w[i].astype(BF16), v_refs[i][...].astype(BF16)) for i in range(n)]
        for i in range(n):
            acc = acc + pv[i]
        return carry, acc

    def store_state(carry, acc):
        carry_ref[...] = carry
        acc_ref[...] = acc

    @pl.when(pg == 0)
    def _():
        t = lax.broadcasted_iota(jnp.int32, (rows, PAGE_SIZE), 0) // heads
        s = lax.broadcasted_iota(jnp.int32, (rows, PAGE_SIZE), 1)
        store_state(*sweep(kn, vn, jnp.zeros((rows, PAGE_SIZE), F32), jnp.zeros(acc_ref.shape, F32), s < t))

    store_state(*sweep(kp, vp, carry_ref[...], acc_ref[...], None))

    @pl.when(pg == pl.num_programs(1) - 1)
    def _():
        a = acc_ref[...] * hm_ref[...]
        o = jnp.sum(a.reshape(rows // heads, heads, a.shape[1]), axis=1)
        ms = _dot_exact_rhs(o * o, bd_ref[...]) * (1.0 / HEAD_DIM)
        o_ref[0] = (o * lax.rsqrt(ms + NORM_EPS) * g_ref[...]).astype(BF16)


def sb_sample(q, k_new, v_new, bias, g_sb, cache_k, cache_v, layer, page_table):
    b, t, heads, dh = q.shape
    w = heads * dh
    tp = SAMPLE_T_PAD
    pps = PAGES_PER_STEP
    n_pages = page_table.shape[1]
    rows = tp * heads
    keys_last = lambda x: jnp.moveaxis(x, -3, -1).reshape(x.shape[:-3] + (w, x.shape[-3]))
    new_page = lambda x: jnp.pad(keys_last(x), ((0, 0), (0, 0), (0, PAGE_SIZE - t)))
    head_of_col = jnp.arange(w) // dh
    head_of_row = jnp.arange(rows) % heads
    hmask = head_of_row[:, None] == head_of_col[None, :]
    qp = jnp.pad(q.reshape(b, t, w), ((0, 0), (0, tp - t), (0, 0)))
    qx = jnp.where(hmask[None], jnp.repeat(qp, heads, axis=1), jnp.zeros((), BF16))
    bias_full = jnp.broadcast_to(bias[head_of_row][:, None], (rows, PAGE_SIZE)).astype(F32)
    kern = functools.partial(_sb_sample_kernel, heads=heads, pps=pps)
    c2 = lambda i, p, pt: (0, 0)
    per_b = lambda i, p, pt: (i, 0, 0)
    new_spec = pl.BlockSpec((None, w, PAGE_SIZE), per_b)
    page = lambda s: (lambda i, p, pt: (layer, pt[i, n_pages - 1 - (p * pps + s)], 0, 0))
    page_specs = [pl.BlockSpec((None, None, w, PAGE_SIZE), page(s)) for s in range(pps)]
    grid_spec = pltpu.PrefetchScalarGridSpec(
        num_scalar_prefetch=1,
        grid=(b, n_pages // pps),
        in_specs=[
            pl.BlockSpec((1, rows, w), per_b),
            pl.BlockSpec((rows, PAGE_SIZE), c2),
            pl.BlockSpec((1, w), c2),
            pl.BlockSpec((PAGE_SIZE, PAGE_SIZE), c2),
            pl.BlockSpec((PAGE_SIZE, PAGE_SIZE), c2),
            pl.BlockSpec((rows, w), c2),
            pl.BlockSpec((w, w), c2),
            new_spec, new_spec,
        ] + page_specs + page_specs,
        out_specs=pl.BlockSpec((1, tp, w), per_b),
        scratch_shapes=[
            pltpu.VMEM((rows, PAGE_SIZE), F32),
            pltpu.VMEM((rows, w), F32),
        ],
    )
    o = pl.pallas_call(
        kern,
        grid_spec=grid_spec,
        out_shape=jax.ShapeDtypeStruct((b, tp, w), BF16),
        compiler_params=_params("parallel", "arbitrary"),
        name="sb_sample",
    )(page_table, qx, bias_full, g_sb, _later_matrix(PAGE_SIZE), jnp.ones((PAGE_SIZE, PAGE_SIZE), BF16),
      hmask.astype(F32), _head_block_ones(w), new_page(k_new), new_page(v_new),
      *([keys_last(cache_k)] * pps), *([keys_last(cache_v)] * pps))
    return o[:, :t].reshape(b, t, heads, dh)


def _rwkv_prep_kernel(*refs, rw, halo_mode, tiles_per_seq):
    if halo_mode:
        u_ref, halo_ref, st_ref = refs[:3]
        rest = refs[3:]
    else:
        u_ref, up_ref = refs[:2]
        rest = refs[2:]
    (mu_ref, w0_ref, w2h_ref, w2l_ref, a0_ref, a2h_ref, a2l_ref, g2h_ref, g2l_ref,
     kkw_ref, ka_ref, rk_ref, bd_ref,
     r_ref, k_ref, v_ref, lw_ref, kk_ref, b_ref, gate_ref, bonus_ref) = rest
    u = u_ref[...]
    if halo_mode:
        tm = u.shape[0]
        row = lax.broadcasted_iota(jnp.int32, (tm, 1), 0)
        first = (pl.program_id(0) % tiles_per_seq) == 0
        prev_row = jnp.where(first, st_ref[0], halo_ref[7:8, :])
        up = jnp.where(row == 0, prev_row, pltpu.roll(u, 1, axis=0))
    else:
        up = up_ref[...]
    us = u + (up - u) * mu_ref[...]
    r = us[:, 0:rw]
    k = us[:, rw:2 * rw]
    v = us[:, 2 * rw:3 * rw]
    o = 3 * rw
    w_lo = us[:, o:o + LANES]
    a_lo = us[:, o + LANES:o + 2 * LANES]
    g_lo = us[:, o + 2 * LANES:o + 4 * LANES]
    d = w0_ref[...] + _dot3(jnp.tanh(w_lo), w2h_ref[...], w2l_ref[...])
    w_log = -_softplus(-d) - 0.5
    lw_ref[...] = -jnp.exp(w_log)
    a = _sigmoid(a0_ref[...] + _dot3(a_lo, a2h_ref[...], a2l_ref[...]))
    gate_ref[...] = _dot3(_sigmoid(g_lo), g2h_ref[...], g2l_ref[...])
    kkr = k * kkw_ref[...]
    nrm = jnp.sqrt(_dot_exact_rhs(kkr * kkr, bd_ref[...]))
    kk = kkr / jnp.maximum(nrm, KK_EPS)
    k_eff = k * (1.0 + (a - 1.0) * ka_ref[...])
    r_ref[...] = r
    k_ref[...] = k_eff
    v_ref[...] = v
    kk_ref[...] = kk
    b_ref[...] = kk * a
    bonus_ref[...] = _dot_exact_rhs(r * k_eff * rk_ref[...], bd_ref[...]) * v


def rwkv_prep(u, prev, pw, rw, seq_len, *, tm, halo_mode):
    n, uw = u.shape
    tm = min(tm, n)
    c = lambda i: (0, 0)
    row_spec = pl.BlockSpec((tm, uw), lambda i: (i, 0))
    if halo_mode:
        tps = seq_len // tm
        halo_blk = tm // 8
        lead_specs = [row_spec,
                      pl.BlockSpec((8, uw), lambda i: (jnp.maximum(i * halo_blk - 1, 0), 0)),
                      pl.BlockSpec((1, 1, uw), lambda i: (i // tps, 0, 0))]
        lead = (u, u, prev)
    else:
        tps = 1
        lead_specs = [row_spec, row_spec]
        lead = (u, prev)
    vec = lambda width: pl.BlockSpec((1, width), c)
    mat = lambda rows: pl.BlockSpec((rows, rw), c)
    in_specs = lead_specs + [
        vec(uw), vec(rw), mat(LANES), mat(LANES), vec(rw), mat(LANES), mat(LANES),
        mat(2 * LANES), mat(2 * LANES), vec(rw), vec(rw), vec(rw), pl.BlockSpec((rw, rw), c)]
    out_spec = pl.BlockSpec((tm, rw), lambda i: (i, 0))
    kern = functools.partial(_rwkv_prep_kernel, rw=rw, halo_mode=halo_mode, tiles_per_seq=tps)
    return pl.pallas_call(
        kern,
        grid=(n // tm,),
        in_specs=in_specs,
        out_specs=[out_spec] * 8,
        out_shape=[jax.ShapeDtypeStruct((n, rw), F32)] * 8,
        compiler_params=_params("parallel"),
        name="rwkv_prep",
    )(*lead, pw['mu'], pw['w0'], pw['w2h'], pw['w2l'], pw['a0'], pw['a2h'], pw['a2l'],
      pw['g2h'], pw['g2l'], pw['kk'], pw['ka'], pw['rk'], pw['bd'])


def _rwkv_chunk_kernel(r_ref, k_ref, v_ref, lw_ref, kk_ref, b_ref, y_ref, s_out_ref, s_ref, *, chunk, npairs):
    c = pl.program_id(1)
    C = chunk
    C2 = HEADS_PER_TILE * C

    @pl.when(c == 0)
    def _():
        s_ref[...] = jnp.zeros_like(s_ref)

    ci = lax.broadcasted_iota(jnp.int32, (C, C), 0)
    cj = lax.broadcasted_iota(jnp.int32, (C, C), 1)
    ti = lax.broadcasted_iota(jnp.int32, (C2, C2), 0)
    tj = lax.broadcasted_iota(jnp.int32, (C2, C2), 1)
    same_blk = (ti // C) == (tj // C)
    incl = same_blk & (ti >= tj)
    strict = same_blk & (ti > tj)
    eye = (ti == tj).astype(F32)
    lane_head = lax.broadcasted_iota(jnp.int32, (C2, LANES), 1) // HEAD_DIM
    row_head = lax.broadcasted_iota(jnp.int32, (C2, LANES), 0) // C
    own_lanes = lane_head == row_head

    def stack(x):
        return jnp.where(own_lanes, jnp.concatenate([x] * HEADS_PER_TILE, axis=0), jnp.zeros((), BF16))

    lw = lw_ref[...]
    l_hi, l_mid, l_lo = _split3(lw)
    tri = (ci >= cj).astype(BF16)
    cl = _dot(tri, l_hi) + (_dot(tri, l_mid) + _dot(tri, l_lo))
    cl_end = cl[C - 1:C, :]
    e_pos = jnp.exp(cl)
    e_neg = jnp.exp(-cl)
    e_prev = jnp.exp(cl - lw)
    e_end = jnp.exp(cl_end - cl)
    all_ones = jnp.ones((C, LANES), BF16)
    d_rows = jnp.exp(_dot_tn(l_hi, all_ones) + (_dot_tn(l_mid, all_ones) + _dot_tn(l_lo, all_ones)))
    r = r_ref[...]
    k = k_ref[...]
    b = b_ref[...]
    rt = (r * e_pos).astype(BF16)
    kt = (kk_ref[...] * e_prev).astype(BF16)
    k_inv = (k * e_neg).astype(BF16)
    b_inv = (b * e_neg).astype(BF16)
    k_dec = (k * e_end).astype(BF16)
    b_dec = (b * e_end).astype(BF16)
    vb = v_ref[...].astype(BF16)

    tiles = range(npairs)
    sls = [slice(p * LANES, (p + 1) * LANES) for p in tiles]
    st = [s_ref[p] for p in tiles]
    stb = [s.astype(BF16) for s in st]
    v2 = [stack(vb[:, sl]) for sl in sls]
    kt2 = [stack(kt[:, sl]) for sl in sls]
    rt2 = [stack(rt[:, sl]) for sl in sls]
    ab = [_dot_nt(jnp.concatenate([kt2[p], rt2[p]], axis=0),
                  jnp.concatenate([stack(k_inv[:, sls[p]]), stack(b_inv[:, sls[p]])], axis=0))
          for p in tiles]
    a_m = [jnp.where(strict, m[:C2, :C2], 0.0).astype(BF16) for m in ab]
    n_m = [jnp.where(strict, -m[:C2, C2:], 0.0) for m in ab]
    aq_m = [jnp.where(incl, m[C2:, :C2], 0.0).astype(BF16) for m in ab]
    nbq_m = [jnp.where(incl, -m[C2:, C2:], 0.0).astype(BF16) for m in ab]
    g = [eye + m for m in n_m]
    pw = [_dot(m.astype(BF16), m.astype(BF16)) for m in n_m]
    levels = int(math.log2(C))
    for lvl in range(1, levels):
        if lvl < levels - 1:
            gp = [_dot(jnp.concatenate([g[p], pw[p]], axis=0).astype(BF16), pw[p].astype(BF16)) for p in tiles]
            g = [g[p] + gp[p][:C2] for p in tiles]
            pw = [m[C2:] for m in gp]
        else:
            g = [g[p] + _dot(g[p].astype(BF16), pw[p].astype(BF16)) for p in tiles]
    x = [_dot(jnp.concatenate([a_m[p], kt2[p]], axis=1), jnp.concatenate([v2[p], stb[p]], axis=0)) for p in tiles]
    u2b = [_dot(g[p].astype(BF16), x[p].astype(BF16)).astype(BF16) for p in tiles]
    y2 = [_dot(jnp.concatenate([aq_m[p], nbq_m[p], rt2[p]], axis=1),
               jnp.concatenate([v2[p], u2b[p], stb[p]], axis=0)) for p in tiles]
    upd = [_dot_tn(jnp.concatenate([stack(k_dec[:, sls[p]]), stack(b_dec[:, sls[p]])], axis=0),
                   jnp.concatenate([v2[p], -u2b[p]], axis=0)) for p in tiles]
    for p in tiles:
        y = y2[p][:C]
        for hh in range(1, HEADS_PER_TILE):
            y = y + y2[p][hh * C:(hh + 1) * C]
        y_ref[:, sls[p]] = y
        s_new = st[p] * d_rows[sls[p], :] + upd[p]
        s_ref[p] = s_new
        s_out_ref[0, p] = s_new


def rwkv_chunk(r, k, v, lw, kk, b, batch, seq_len):
    n, rw = r.shape
    npairs = rw // LANES
    C = RW_CHUNK
    nc = seq_len // C
    blk = pl.BlockSpec((C, rw), lambda i, c: (i * nc + c, 0))
    kern = functools.partial(_rwkv_chunk_kernel, chunk=C, npairs=npairs)
    return pl.pallas_call(
        kern,
        grid=(batch, nc),
        in_specs=[blk] * 6,
        out_specs=[blk, pl.BlockSpec((1, npairs, LANES, LANES), lambda i, c: (i, 0, 0, 0))],
        out_shape=[jax.ShapeDtypeStruct((n, rw), F32),
                   jax.ShapeDtypeStruct((batch, npairs, LANES, LANES), F32)],
        scratch_shapes=[pltpu.VMEM((npairs, LANES, LANES), F32)],
        compiler_params=_params("parallel", "arbitrary"),
        name="rwkv_chunk",
    )(r, k, v, lw, kk, b)


def _rwkv_seq_kernel(s0_ref, r_ref, k_ref, lw_ref, kk_ref, b_ref, vcol_ref, ycol_ref, s_out_ref, *, steps):
    s = s0_ref[0]
    for t in range(steps):
        sa = jnp.sum(s * kk_ref[0, t], axis=-1, keepdims=True)
        s = s * jnp.exp(lw_ref[0, t]) - sa * b_ref[0, t] + vcol_ref[0, t] * k_ref[0, t]
        ycol_ref[0, t] = jnp.sum(s * r_ref[0, t], axis=-1, keepdims=True)
    s_out_ref[0] = s


def rwkv_seq(s0, r, k, v, lw, kk, b):
    bsz, heads, dh, _ = s0.shape
    t = r.shape[1]
    rowv = lambda x: x.reshape(bsz, t, heads, 1, dh)
    row_spec = pl.BlockSpec((1, t, heads, 1, dh), lambda i: (i, 0, 0, 0, 0))
    col_spec = pl.BlockSpec((1, t, heads, dh, 1), lambda i: (i, 0, 0, 0, 0))
    st_spec = pl.BlockSpec((1, heads, dh, dh), lambda i: (i, 0, 0, 0))
    kern = functools.partial(_rwkv_seq_kernel, steps=t)
    ycol, s = pl.pallas_call(
        kern,
        grid=(bsz,),
        in_specs=[st_spec] + [row_spec] * 5 + [col_spec],
        out_specs=[col_spec, st_spec],
        out_shape=[jax.ShapeDtypeStruct((bsz, t, heads, dh, 1), F32),
                   jax.ShapeDtypeStruct(s0.shape, F32)],
        compiler_params=_params("parallel"),
        name="rwkv_seq",
    )(s0, rowv(r), rowv(k), rowv(lw), rowv(kk), rowv(b), v.reshape(bsz, t, heads, dh, 1))
    return ycol.reshape(bsz, t, heads * dh), s


def _rwkv_epi_kernel(y_ref, bonus_ref, gate_ref, g_ref, b_ref, bd_ref, o_ref):
    y = y_ref[...]
    inv = 1.0 / HEAD_DIM
    mu = _dot_exact_rhs(y, bd_ref[...]) * inv
    yc = y - mu
    var = _dot_exact_rhs(yc * yc, bd_ref[...]) * inv
    yn = yc * lax.rsqrt(var + LNX_EPS) * g_ref[...] + b_ref[...]
    o_ref[...] = ((yn + bonus_ref[...]) * gate_ref[...]).astype(BF16)


def rwkv_epilogue(y, bonus, gate, ln_g, ln_b, bd, *, tm):
    n, rw = y.shape
    tm = min(tm, n)
    blk = pl.BlockSpec((tm, rw), lambda i: (i, 0))
    vec = pl.BlockSpec((1, rw), lambda i: (0, 0))
    return pl.pallas_call(
        _rwkv_epi_kernel,
        grid=(n // tm,),
        in_specs=[blk, blk, blk, vec, vec, pl.BlockSpec((rw, rw), lambda i: (0, 0))],
        out_specs=blk,
        out_shape=jax.ShapeDtypeStruct((n, rw), BF16),
        compiler_params=_params("parallel"),
        name="rwkv_epilogue",
    )(y, bonus, gate, ln_g, ln_b, bd)


def _pad_rows(m, rows):
    return jnp.pad(m, ((0, rows - m.shape[0]), (0, 0)))


def _hi_lo(m):
    hi = m.astype(BF16)
    return hi, (m - hi.astype(F32)).astype(BF16)


def _u_layout(rw, n_decay, n_icl, n_gate):
    segs = [(3 * rw, 3 * rw), (n_decay, LANES), (n_icl, LANES), (n_gate, 2 * LANES)]
    assert n_decay <= LANES and n_icl <= LANES and n_gate <= 2 * LANES
    return segs


def _pad_u_cols(m, segs):
    out, o = [], 0
    for width, padded in segs:
        out.append(m[..., o:o + width])
        if padded > width:
            out.append(jnp.zeros(m.shape[:-1] + (padded - width,), m.dtype))
        o += width
    return jnp.concatenate(out, axis=-1)


def _unpad_u_cols(m, segs):
    out, o = [], 0
    for width, padded in segs:
        out.append(m[..., o:o + width])
        o += padded
    return jnp.concatenate(out, axis=-1)


def _prepare_weights(w_in, w_out, g_pre_mix, g_post_mix, g_pre_ffn, g_post_ffn, g_sb_out, sb_bias,
                     rw_mu, rw_w0, rw_w2, rw_a0, rw_a2, rw_g2, rw_kk, rw_ka, rw_rk, rw_ln_g, rw_ln_b,
                     w_up, conv_w, conv_b, w_down):
    rw = rw_w0.shape[0]
    sbw = g_sb_out.shape[0]
    segs = _u_layout(rw, rw_w2.shape[0], rw_a2.shape[0], rw_g2.shape[0])
    row = lambda x: x.reshape(1, -1)
    w2h, w2l = _hi_lo(_pad_rows(rw_w2, LANES))
    a2h, a2l = _hi_lo(_pad_rows(rw_a2, LANES))
    g2h, g2l = _hi_lo(_pad_rows(rw_g2, 2 * LANES))
    bd = _head_block_ones(rw)
    w_in_p = jnp.concatenate([w_in[:, :3 * sbw], _pad_u_cols(w_in[:, 3 * sbw:], segs)], axis=1).astype(BF16)
    prep = dict(mu=row(_pad_u_cols(rw_mu, segs)), w0=row(rw_w0), w2h=w2h, w2l=w2l, a0=row(rw_a0),
                a2h=a2h, a2l=a2l, g2h=g2h, g2l=g2l, kk=row(rw_kk), ka=row(rw_ka),
                rk=row(rw_rk), bd=bd)
    return dict(
        segs=segs, rw=rw, sbw=sbw, w_in=w_in_p, w_out=w_out.astype(BF16),
        g_pre_mix=row(g_pre_mix), g_post_mix=row(g_post_mix), g_pre_ffn=row(g_pre_ffn),
        g_post_ffn=row(g_post_ffn), g_sb=row(g_sb_out), sb_bias=sb_bias, prep=prep,
        ln_g=row(rw_ln_g), ln_b=row(rw_ln_b), bd=bd,
        w_up=w_up.astype(BF16), conv_w=conv_w, conv_b=row(conv_b), w_down=w_down.astype(BF16))


def _tail(wts, x, o_sb, o_rw, *, tm):
    mixed_in = jnp.concatenate([o_sb, o_rw], axis=-1)
    return mm_norm_resid(mixed_in, wts['w_out'], x, wts['g_post_mix'], wts['g_pre_ffn'],
                         tm=tm, tk=mixed_in.shape[1], emit_h=True)


def _down(wts, act, x1, *, tm):
    (y,) = mm_norm_resid(act, wts['w_down'], x1, wts['g_post_ffn'], wts['g_post_ffn'],
                         tm=tm, tk=act.shape[1] // 4, emit_h=False)
    return y


def _prompt_layer(wts, x3):
    bsz, t, d = x3.shape
    n = bsz * t
    x = x3.reshape(n, d)
    rw, sbw, segs = wts['rw'], wts['sbw'], wts['segs']
    uw = sum(p for _, p in segs)
    q, k, v, u = in_proj(x, wts['g_pre_mix'], wts['w_in'], sbw, uw, tm=1024)
    o_sb = sb_prompt(q, k, v, wts['sb_bias'], wts['g_sb'], bsz, t, tq=min(512, t), tk=min(256, t))
    shift0 = jnp.zeros((bsz, 1, uw), F32)
    r, ke, vr, lw, kk, b, gate, bonus = rwkv_prep(u, shift0, wts['prep'], rw, t, tm=min(256, t), halo_mode=True)
    y, s_pairs = rwkv_chunk(r, ke, vr, lw, kk, b, bsz, t)
    o_rw = rwkv_epilogue(y, bonus, gate, wts['ln_g'], wts['ln_b'], wts['bd'], tm=512)
    x1, h2 = _tail(wts, x, o_sb, o_rw, tm=512)
    tm_f = min(1024, t)
    conv0 = jnp.zeros((bsz, CONV_W - 1, wts['conv_w'].shape[1]), F32)
    act, gtail = ffn_up_prompt(h2, wts['w_up'], wts['conv_w'], wts['conv_b'], conv0, t, tm=tm_f)
    yout = _down(wts, act, x1, tm=512)
    heads = rw // HEAD_DIM
    hp = HEADS_PER_TILE
    sp = s_pairs.reshape(bsz, rw // LANES, hp, HEAD_DIM, hp, HEAD_DIM)
    wkv = jnp.stack([jnp.swapaxes(sp[:, :, i, :, i, :], -1, -2) for i in range(hp)], axis=2)
    wkv = wkv.reshape(bsz, heads, HEAD_DIM, HEAD_DIM)
    shift = _unpad_u_cols(u.reshape(bsz, t, uw)[:, -1], segs)
    tps = t // tm_f
    conv_new = gtail.reshape(bsz, tps, 8, -1)[:, -1, 8 - (CONV_W - 1):]
    kvshape = (bsz, t, sbw // HEAD_DIM, HEAD_DIM)
    return yout.reshape(bsz, t, d), k.reshape(kvshape), v.reshape(kvshape), wkv, shift, conv_new


def _sample_layer(wts, x3, cache_k, cache_v, layer, page_table, wkv0, shift_prev, conv_prev):
    bsz, t, d = x3.shape
    n = bsz * t
    x = x3.reshape(n, d)
    rw, sbw, segs = wts['rw'], wts['sbw'], wts['segs']
    uw = sum(p for _, p in segs)
    q, k, v, u = in_proj(x, wts['g_pre_mix'], wts['w_in'], sbw, uw, tm=n)
    kvshape = (bsz, t, sbw // HEAD_DIM, HEAD_DIM)
    o_sb = sb_sample(q.reshape(kvshape), k.reshape(kvshape), v.reshape(kvshape),
                     wts['sb_bias'], wts['g_sb'], cache_k, cache_v, layer, page_table).reshape(n, sbw)
    u3 = u.reshape(bsz, t, uw)
    u_prev = jnp.concatenate([_pad_u_cols(shift_prev, segs)[:, None], u3[:, :-1]], axis=1).reshape(n, uw)
    r, ke, vr, lw, kk, b, gate, bonus = rwkv_prep(u, u_prev, wts['prep'], rw, t, tm=n, halo_mode=False)
    as3 = lambda a: a.reshape(bsz, t, rw)
    y3, wkv = rwkv_seq(wkv0, as3(r), as3(ke), as3(vr), as3(lw), as3(kk), as3(b))
    o_rw = rwkv_epilogue(y3.reshape(n, rw), bonus, gate, wts['ln_g'], wts['ln_b'], wts['bd'], tm=n)
    x1, h2 = _tail(wts, x, o_sb, o_rw, tm=n)
    f = wts['conv_w'].shape[1]
    zeros = jnp.zeros((bsz, t, f), F32)
    p1 = zeros.at[:, 0].set(conv_prev[:, 1]).reshape(n, f)
    p2 = zeros.at[:, 0].set(conv_prev[:, 0]).at[:, 1].set(conv_prev[:, 1]).reshape(n, f)
    act, gfull = ffn_up_sample(h2, wts['w_up'], wts['conv_w'], wts['conv_b'], p1, p2, t)
    yout = _down(wts, act, x1, tm=n)
    shift = _unpad_u_cols(u3[:, -1], segs)
    conv_new = gfull.reshape(bsz, t, f)[:, t - (CONV_W - 1):]
    kvshape = (bsz, t, sbw // HEAD_DIM, HEAD_DIM)
    return yout.reshape(bsz, t, d), k.reshape(kvshape), v.reshape(kvshape), wkv, shift, conv_new


def kernel(x_prompt, x_sample, cache_sb_k, cache_sb_v, state_rwkv_wkv, state_rwkv_shift, state_ffn_conv, page_table, w_in, w_out, g_pre_mix, g_post_mix, g_pre_ffn, g_post_ffn, g_sb_out, sb_bias, rw_mu, rw_w0, rw_w2, rw_a0, rw_a2, rw_g2, rw_kk, rw_ka, rw_rk, rw_ln_g, rw_ln_b, w_up, conv_w, conv_b, w_down):
    depth = w_in.shape[0]
    yp, ys = x_prompt, x_sample
    outs_p, outs_s = [], []
    for l in range(depth):
        wts = _prepare_weights(w_in[l], w_out[l], g_pre_mix[l], g_post_mix[l], g_pre_ffn[l], g_post_ffn[l],
                               g_sb_out[l], sb_bias[l], rw_mu[l], rw_w0[l], rw_w2[l], rw_a0[l], rw_a2[l],
                               rw_g2[l], rw_kk[l], rw_ka[l], rw_rk[l].reshape(-1), rw_ln_g[l], rw_ln_b[l],
                               w_up[l], conv_w[l], conv_b[l], w_down[l])
        yp, *rest_p = _prompt_layer(wts, yp)
        ys, *rest_s = _sample_layer(wts, ys, cache_sb_k, cache_sb_v, l, page_table,
                                    state_rwkv_wkv[l], state_rwkv_shift[l], state_ffn_conv[l])
        outs_p.append(rest_p)
        outs_s.append(rest_s)
    stack = lambda outs, i: jnp.stack([o[i] for o in outs])
    return (yp, ys,
            stack(outs_p, 0), stack(outs_p, 1), stack(outs_p, 2), stack(outs_p, 3), stack(outs_p, 4),
            stack(outs_s, 0), stack(outs_s, 1), stack(outs_s, 2), stack(outs_s, 3), stack(outs_s, 4))
```

```python
import functools
import math

import jax
import jax.numpy as jnp
from jax import lax
from jax.experimental import pallas as pl
from jax.experimental.pallas import tpu as pltpu

F32 = jnp.float32
BF16 = jnp.bfloat16

HEAD_DIM = 64
LANES = 128
HEADS_PER_TILE = LANES // HEAD_DIM
NORM_EPS = 1e-6
LNX_EPS = 64e-5
KK_EPS = 1e-12
CONV_W = 3
PAGE_SIZE = 128
VMEM_LIMIT = 48 * 1024 * 1024
RW_CHUNK = 64


def _dot(a, b):
    return jnp.dot(a, b, preferred_element_type=F32)


def _dot_nt(a, b):
    return lax.dot_general(a, b, (((1,), (1,)), ((), ())), preferred_element_type=F32)


def _dot_tn(a, b):
    return lax.dot_general(a, b, (((0,), (0,)), ((), ())), preferred_element_type=F32)


def _split2(x):
    hi = x.astype(BF16)
    lo = (x - hi.astype(F32)).astype(BF16)
    return hi, lo


def _split3(x):
    hi = x.astype(BF16)
    r1 = x - hi.astype(F32)
    mid = r1.astype(BF16)
    lo = (r1 - mid.astype(F32)).astype(BF16)
    return hi, mid, lo


def _dot_exact_rhs(x, m01):
    hi, lo = _split2(x)
    return _dot(hi, m01) + _dot(lo, m01)


def _dot3(a, b_hi, b_lo):
    a_hi, a_lo = _split2(a)
    return _dot(a_hi, b_hi) + (_dot(a_lo, b_hi) + _dot(a_hi, b_lo))


def _softplus(x):
    return jnp.maximum(x, 0.0) + jnp.log(1.0 + jnp.exp2(jnp.abs(x) * (-math.log2(math.e))))


def _sigmoid(x):
    return 1.0 / (1.0 + jnp.exp(-x))


def _rms_rows(x, g):
    return x * lax.rsqrt(jnp.mean(x * x, axis=-1, keepdims=True) + NORM_EPS) * g


def _params(*sem):
    return pltpu.CompilerParams(dimension_semantics=sem, vmem_limit_bytes=VMEM_LIMIT)


def _head_block_ones(n):
    i = jnp.arange(n) // HEAD_DIM
    return (i[:, None] == i[None, :]).astype(BF16)


def _in_proj_kernel(x_ref, g_ref, w_ref, q_ref, k_ref, v_ref, u_ref, h_ref, *, nq, nk, nv):
    j = pl.program_id(1)

    @pl.when(j == 0)
    def _():
        h_ref[...] = _rms_rows(x_ref[...], g_ref[...]).astype(BF16)

    res = _dot(h_ref[...], w_ref[...])

    @pl.when(j < nq)
    def _():
        q_ref[...] = (res * (HEAD_DIM ** -0.5)).astype(BF16)

    @pl.when((j >= nq) & (j < nq + nk))
    def _():
        k_ref[...] = res

    @pl.when((j >= nq + nk) & (j < nq + nk + nv))
    def _():
        v_ref[...] = res

    @pl.when(j >= nq + nk + nv)
    def _():
        u_ref[...] = res


def in_proj(x, g, w, sb_width, u_width, *, tm, tn=512):
    n, d = x.shape
    tm = min(tm, n)
    nq = nk = nv = sb_width // tn
    nu = u_width // tn
    nj = nq + nk + nv + nu
    off_k, off_v, off_u = nq, nq + nk, nq + nk + nv
    clip = lambda j, lo, cnt: jnp.clip(j - lo, 0, cnt - 1)
    kern = functools.partial(_in_proj_kernel, nq=nq, nk=nk, nv=nv)
    return pl.pallas_call(
        kern,
        grid=(n // tm, nj),
        in_specs=[
            pl.BlockSpec((tm, d), lambda i, j: (i, 0)),
            pl.BlockSpec((1, d), lambda i, j: (0, 0)),
            pl.BlockSpec((d, tn), lambda i, j: (0, j)),
        ],
        out_specs=[
            pl.BlockSpec((tm, tn), lambda i, j: (i, clip(j, 0, nq))),
            pl.BlockSpec((tm, tn), lambda i, j: (i, clip(j, off_k, nk))),
            pl.BlockSpec((tm, tn), lambda i, j: (i, clip(j, off_v, nv))),
            pl.BlockSpec((tm, tn), lambda i, j: (i, clip(j, off_u, nu))),
        ],
        out_shape=[
            jax.ShapeDtypeStruct((n, sb_width), BF16),
            jax.ShapeDtypeStruct((n, sb_width), F32),
            jax.ShapeDtypeStruct((n, sb_width), F32),
            jax.ShapeDtypeStruct((n, u_width), F32),
        ],
        scratch_shapes=[pltpu.VMEM((tm, d), BF16)],
        compiler_params=_params("parallel", "arbitrary"),
        name="in_proj",
    )(x, g, w)


def _mm_norm_resid_kernel(*refs, n_parts, nk, emit_h):
    a_refs = refs[:n_parts]
    w_ref, r_ref, g_ref, g2_ref = refs[n_parts:n_parts + 4]
    rest = refs[n_parts + 4:]
    if emit_h:
        o_ref, h_ref, acc_ref = rest
    else:
        o_ref, acc_ref = rest
        h_ref = None
    kk = pl.program_id(1)

    def finalize(m):
        y = r_ref[...] + _rms_rows(m, g_ref[...])
        o_ref[...] = y
        if emit_h:
            h_ref[...] = _rms_rows(y, g2_ref[...]).astype(BF16)

    def k_step(j):
        part = _dot(a_refs[j if n_parts > 1 else 0][...], w_ref[...])
        if nk == 1:
            finalize(part)
        elif j == 0:
            acc_ref[...] = part
        elif j < nk - 1:
            acc_ref[...] += part
        else:
            finalize(acc_ref[...] + part)

    if nk == 1:
        k_step(0)
    else:
        for j in range(nk):
            pl.when(kk == j)(functools.partial(k_step, j))


def mm_norm_resid(a, w, resid, g, g2, *, tm, tk, emit_h):
    parts = a if isinstance(a, tuple) else (a,)
    n = parts[0].shape[0]
    kdim, d = w.shape
    tm = min(tm, n)
    nk = kdim // tk
    assert len(parts) == 1 or (len(parts) == nk and all(p.shape[1] == tk for p in parts))
    kern = functools.partial(_mm_norm_resid_kernel, n_parts=len(parts), nk=nk, emit_h=emit_h)
    out_specs = [pl.BlockSpec((tm, d), lambda i, k: (i, 0))]
    out_shape = [jax.ShapeDtypeStruct((n, d), F32)]
    if emit_h:
        out_specs.append(pl.BlockSpec((tm, d), lambda i, k: (i, 0)))
        out_shape.append(jax.ShapeDtypeStruct((n, d), BF16))
    a_index = (lambda i, k: (i, k)) if len(parts) == 1 else (lambda i, k: (i, 0))
    return pl.pallas_call(
        kern,
        grid=(n // tm, nk),
        in_specs=[pl.BlockSpec((tm, tk), a_index)] * len(parts) + [
            pl.BlockSpec((tk, d), lambda i, k: (k, 0)),
            pl.BlockSpec((tm, d), lambda i, k: (i, 0)),
            pl.BlockSpec((1, d), lambda i, k: (0, 0)),
            pl.BlockSpec((1, d), lambda i, k: (0, 0)),
        ],
        out_specs=out_specs,
        out_shape=out_shape,
        scratch_shapes=[pltpu.VMEM((tm, d), F32)],
        compiler_params=_params("parallel", "arbitrary"),
        name="mm_norm_resid",
    )(*parts, w, resid, g, g2)


def _gelu_tanh(x):
    c = math.sqrt(2.0 / math.pi)
    return x * (0.5 * (1.0 + jnp.tanh(c * (x + 0.044715 * (x * x * x)))))


def _ffn_up_kernel(*refs, tiles_per_seq, seq_len, halo_mode):
    if halo_mode:
        h_ref, halo_ref, wg_ref, wv_ref, cw_ref, cb_ref, prev_ref, act_ref, g_out_ref = refs
    else:
        h_ref, wg_ref, wv_ref, cw_ref, cb_ref, p1_ref, p2_ref, act_ref, g_out_ref = refs
    h = h_ref[...]
    g = _dot(h, wg_ref[...])
    val = _dot(h, wv_ref[...])
    tm = g.shape[0]
    row = lax.broadcasted_iota(jnp.int32, (tm, 1), 0)
    r1 = pltpu.roll(g, 1, axis=0)
    r2 = pltpu.roll(g, 2, axis=0)
    if halo_mode:
        gh = _dot(halo_ref[...], wg_ref[...])
        first = (pl.program_id(0) % tiles_per_seq) == 0
        pm1 = jnp.where(first, prev_ref[0, 1:2, :], gh[7:8, :])
        pm2 = jnp.where(first, prev_ref[0, 0:1, :], gh[6:7, :])
        g1 = jnp.where(row == 0, pm1, r1)
        g2 = jnp.where(row == 0, pm2, jnp.where(row == 1, pm1, r2))
        g_out_ref[0] = g[tm - 8:, :]
    else:
        t = row % seq_len
        g1 = jnp.where(t >= 1, r1, p1_ref[...])
        g2 = jnp.where(t >= 2, r2, p2_ref[...])
        g_out_ref[...] = g
    conv = cb_ref[...] + cw_ref[0:1, :] * g2 + cw_ref[1:2, :] * g1 + cw_ref[2:3, :] * g
    act_ref[...] = (_gelu_tanh(conv) * val).astype(BF16)


def ffn_up_prompt(h, w_up, conv_w, conv_b, conv_prev, seq_len, *, tm, tn=512):
    n, d = h.shape
    f = w_up.shape[1] // 2
    nf = f // tn
    tps = seq_len // tm
    kern = functools.partial(_ffn_up_kernel, tiles_per_seq=tps, seq_len=seq_len, halo_mode=True)
    halo_blk = tm // 8
    return pl.pallas_call(
        kern,
        grid=(n // tm, nf),
        in_specs=[
            pl.BlockSpec((tm, d), lambda i, j: (i, 0)),
            pl.BlockSpec((8, d), lambda i, j: (jnp.maximum(i * halo_blk - 1, 0), 0)),
            pl.BlockSpec((d, tn), lambda i, j: (0, j)),
            pl.BlockSpec((d, tn), lambda i, j: (0, j + nf)),
            pl.BlockSpec((CONV_W, tn), lambda i, j: (0, j)),
            pl.BlockSpec((1, tn), lambda i, j: (0, j)),
            pl.BlockSpec((1, CONV_W - 1, tn), lambda i, j: (i // tps, 0, j)),
        ],
        out_specs=[
            pl.BlockSpec((tm, tn), lambda i, j: (i, j)),
            pl.BlockSpec((1, 8, tn), lambda i, j: (i, 0, j)),
        ],
        out_shape=[
            jax.ShapeDtypeStruct((n, f), BF16),
            jax.ShapeDtypeStruct((n // tm, 8, f), F32),
        ],
        compiler_params=_params("parallel", "arbitrary"),
        name="ffn_up_prompt",
    )(h, h, w_up, w_up, conv_w, conv_b, conv_prev)


def ffn_up_sample(h, w_up, conv_w, conv_b, p1, p2, seq_len, *, tn=512):
    n, d = h.shape
    f = w_up.shape[1] // 2
    nf = f // tn
    kern = functools.partial(_ffn_up_kernel, tiles_per_seq=1, seq_len=seq_len, halo_mode=False)
    return pl.pallas_call(
        kern,
        grid=(1, nf),
        in_specs=[
            pl.BlockSpec((n, d), lambda i, j: (0, 0)),
            pl.BlockSpec((d, tn), lambda i, j: (0, j)),
            pl.BlockSpec((d, tn), lambda i, j: (0, j + nf)),
            pl.BlockSpec((CONV_W, tn), lambda i, j: (0, j)),
            pl.BlockSpec((1, tn), lambda i, j: (0, j)),
            pl.BlockSpec((n, tn), lambda i, j: (0, j)),
            pl.BlockSpec((n, tn), lambda i, j: (0, j)),
        ],
        out_specs=[
            pl.BlockSpec((n, tn), lambda i, j: (0, j)),
            pl.BlockSpec((n, tn), lambda i, j: (0, j)),
        ],
        out_shape=[
            jax.ShapeDtypeStruct((n, f), BF16),
            jax.ShapeDtypeStruct((n, f), F32),
        ],
        compiler_params=_params("arbitrary", "arbitrary"),
        name="ffn_up_sample",
    )(h, w_up, w_up, conv_w, conv_b, p1, p2)


def _sb_prompt_kernel(bias_ref, q_ref, k_ref, v_ref, later_ref, bd_ref, g_ref, o_ref,
                      kb_ref, vb_ref, carry_ref, acc_ref, *, tq, tk):
    p = pl.program_id(1)
    qi = pl.program_id(2)
    ratio = tq // tk

    @pl.when(qi == 0)
    def _():
        kb_ref[...] = k_ref[...].astype(BF16)
        vb_ref[...] = v_ref[...].astype(BF16)

    q = q_ref[...]
    lane = lax.broadcasted_iota(jnp.int32, (1, LANES), 1)
    qms = [jnp.where((lane // HEAD_DIM) == hh, q, jnp.zeros_like(q)) for hh in range(HEADS_PER_TILE)]
    biases = [bias_ref[p * HEADS_PER_TILE + hh] for hh in range(HEADS_PER_TILE)]
    carry_ref[...] = jnp.zeros_like(carry_ref)
    acc_ref[...] = jnp.zeros_like(acc_ref)
    later = later_ref[...]

    hs = range(HEADS_PER_TILE)

    def key_tiles(jq, masked):
        order = range(ratio - 1, -1, -1)
        j0 = [pl.multiple_of((jq * ratio + d) * tk, tk) for d in order]
        n = len(j0)
        kblk = [kb_ref[pl.ds(s, tk), :] for s in j0]
        vblk = [vb_ref[pl.ds(s, tk), :] for s in j0]
        mask = [None] * n
        if masked:
            qpos = qi * tq + lax.broadcasted_iota(jnp.int32, (tq, tk), 0)
            kcol = lax.broadcasted_iota(jnp.int32, (tq, tk), 1)
            mask = [s + kcol < qpos for s in j0]
        z = [[_dot_nt(qms[hh], kblk[i]) + biases[hh] for hh in hs] for i in range(n)]
        sp = [[_softplus(z[i][hh]) for hh in hs] for i in range(n)]
        if masked:
            sp = [[jnp.where(mask[i], s, 0.0) for s in sp[i]] for i in range(n)]
        ls = [[z[i][hh] - sp[i][hh] for hh in hs] for i in range(n)]
        after = [[_dot(s.astype(BF16), later) for s in sp[i]] for i in range(n)]
        total = [[jnp.sum(s, axis=-1, keepdims=True) for s in sp[i]] for i in range(n)]
        c = [carry_ref[hh] for hh in hs]
        pv = [None] * HEADS_PER_TILE
        for i in range(n):
            for hh in hs:
                w = jnp.exp(ls[i][hh] - after[i][hh] - jnp.concatenate([c[hh]] * (tk // LANES), axis=1))
                if masked:
                    w = jnp.where(mask[i], w, 0.0)
                term = _dot(w.astype(BF16), vblk[i])
                pv[hh] = term if pv[hh] is None else pv[hh] + term
                c[hh] = c[hh] + total[i][hh]
        for hh in hs:
            acc_ref[hh] += pv[hh]
            carry_ref[hh] = c[hh]

    key_tiles(qi, True)

    def body(it, c):
        key_tiles(qi - 1 - it, False)
        return c

    lax.fori_loop(0, qi, body, 0)
    o = acc_ref[0]
    for hh in range(1, HEADS_PER_TILE):
        o = jnp.where((lane // HEAD_DIM) == hh, acc_ref[hh], o)
    ms = _dot_exact_rhs(o * o, bd_ref[...]) * (1.0 / HEAD_DIM)
    o_ref[...] = (o * lax.rsqrt(ms + NORM_EPS) * g_ref[...]).astype(BF16)


def _later_matrix(n):
    idx = jnp.arange(n)
    return (idx[:, None] > idx[None, :]).astype(BF16)


def sb_prompt(q, k, v, bias, g_sb, batch, seq_len, *, tq, tk):
    n, w = q.shape
    npairs = w // LANES
    nq = seq_len // tq
    kern = functools.partial(_sb_prompt_kernel, tq=tq, tk=tk)
    const = lambda b, p, i: (0, 0)
    return pl.pallas_call(
        kern,
        grid=(batch, npairs, nq),
        in_specs=[
            pl.BlockSpec(memory_space=pltpu.SMEM),
            pl.BlockSpec((tq, LANES), lambda b, p, i: (b * nq + i, p)),
            pl.BlockSpec((seq_len, LANES), lambda b, p, i: (b, p)),
            pl.BlockSpec((seq_len, LANES), lambda b, p, i: (b, p)),
            pl.BlockSpec((tk, tk), const),
            pl.BlockSpec((LANES, LANES), const),
            pl.BlockSpec((1, LANES), lambda b, p, i: (0, p)),
        ],
        out_specs=pl.BlockSpec((tq, LANES), lambda b, p, i: (b * nq + i, p)),
        out_shape=jax.ShapeDtypeStruct((n, w), BF16),
        scratch_shapes=[
            pltpu.VMEM((seq_len, LANES), BF16),
            pltpu.VMEM((seq_len, LANES), BF16),
            pltpu.VMEM((HEADS_PER_TILE, tq, LANES), F32),
            pltpu.VMEM((HEADS_PER_TILE, tq, LANES), F32),
        ],
        compiler_params=_params("parallel", "parallel", "arbitrary"),
        name="sb_prompt",
    )(bias, q, k, v, _later_matrix(tk), _head_block_ones(LANES), g_sb)


SAMPLE_T_PAD = 8
PAGES_PER_STEP = 8


def _sb_sample_kernel(pt_ref, q_ref, bias_ref, g_ref, later_ref, ones_ref, hm_ref, bd_ref, *rest, heads, pps):
    del pt_ref
    kn, vn = rest[0:1], rest[1:2]
    kp, vp = rest[2:2 + pps], rest[2 + pps:2 + 2 * pps]
    o_ref, carry_ref, acc_ref = rest[2 + 2 * pps:]
    pg = pl.program_id(1)
    qx = q_ref[0]
    rows = qx.shape[0]
    bias = bias_ref[...]
    later = later_ref[...]
    ones = ones_ref[...]

    def sweep(k_refs, v_refs, carry, acc, mask):
        n = len(k_refs)
        z = [_dot(qx, k_refs[i][...].astype(BF16)) + bias for i in range(n)]
        sp = [_softplus(x) for x in z]
        if mask is not None:
            sp = [jnp.where(mask, s, 0.0) for s in sp]
        spb = [s.astype(BF16) for s in sp]
        after = [_dot(s, later) for s in spb]
        total = [_dot(s, ones) for s in spb]
        w = []
        for i in range(n):
            wi = jnp.exp(z[i] - sp[i] - after[i] - carry)
            w.append(wi if mask is None else jnp.where(mask, wi, 0.0))
            carry = carry + total[i]
        pv = [_dot_nt(w[i].astype(BF16), v_refs[i][...].astype(BF16)) for i in range(n)]
        for i in range(n):
            acc = acc + pv[i]
        return carry, acc

    def store_state(carry, acc):
        carry_ref[...] = carry
        acc_ref[...] = acc

    @pl.when(pg == 0)
    def _():
        t = lax.broadcasted_iota(jnp.int32, (rows, PAGE_SIZE), 0) // heads
        s = lax.broadcasted_iota(jnp.int32, (rows, PAGE_SIZE), 1)
        store_state(*sweep(kn, vn, jnp.zeros((rows, PAGE_SIZE), F32), jnp.zeros(acc_ref.shape, F32), s < t))

    store_state(*sweep(kp, vp, carry_ref[...], acc_ref[...], None))

    @pl.when(pg == pl.num_programs(1) - 1)
    def _():
        a = acc_ref[...] * hm_ref[...]
        o = jnp.sum(a.reshape(rows // heads, heads, a.shape[1]), axis=1)
        ms = _dot_exact_rhs(o * o, bd_ref[...]) * (1.0 / HEAD_DIM)
        o_ref[0] = (o * lax.rsqrt(ms + NORM_EPS) * g_ref[...]).astype(BF16)


def sb_sample(q, k_new, v_new, bias, g_sb, cache_k, cache_v, layer, page_table):
    b, t, heads, dh = q.shape
    w = heads * dh
    tp = SAMPLE_T_PAD
    pps = PAGES_PER_STEP
    n_pages = page_table.shape[1]
    rows = tp * heads
    keys_last = lambda x: jnp.moveaxis(x, -3, -1).reshape(x.shape[:-3] + (w, x.shape[-3]))
    new_page = lambda x: jnp.pad(keys_last(x), ((0, 0), (0, 0), (0, PAGE_SIZE - t)))
    head_of_col = jnp.arange(w) // dh
    head_of_row = jnp.arange(rows) % heads
    hmask = head_of_row[:, None] == head_of_col[None, :]
    qp = jnp.pad(q.reshape(b, t, w), ((0, 0), (0, tp - t), (0, 0)))
    qx = jnp.where(hmask[None], jnp.repeat(qp, heads, axis=1), jnp.zeros((), BF16))
    bias_full = jnp.broadcast_to(bias[head_of_row][:, None], (rows, PAGE_SIZE)).astype(F32)
    kern = functools.partial(_sb_sample_kernel, heads=heads, pps=pps)
    c2 = lambda i, p, pt: (0, 0)
    per_b = lambda i, p, pt: (i, 0, 0)
    new_spec = pl.BlockSpec((None, w, PAGE_SIZE), per_b)
    page = lambda s: (lambda i, p, pt: (layer, pt[i, n_pages - 1 - (p * pps + s)], 0, 0))
    page_specs = [pl.BlockSpec((None, None, w, PAGE_SIZE), page(s)) for s in range(pps)]
    grid_spec = pltpu.PrefetchScalarGridSpec(
        num_scalar_prefetch=1,
        grid=(b, n_pages // pps),
        in_specs=[
            pl.BlockSpec((1, rows, w), per_b),
            pl.BlockSpec((rows, PAGE_SIZE), c2),
            pl.BlockSpec((1, w), c2),
            pl.BlockSpec((PAGE_SIZE, PAGE_SIZE), c2),
            pl.BlockSpec((PAGE_SIZE, PAGE_SIZE), c2),
            pl.BlockSpec((rows, w), c2),
            pl.BlockSpec((w, w), c2),
            new_spec, new_spec,
        ] + page_specs + page_specs,
        out_specs=pl.BlockSpec((1, tp, w), per_b),
        scratch_shapes=[
            pltpu.VMEM((rows, PAGE_SIZE), F32),
            pltpu.VMEM((rows, w), F32),
        ],
    )
    o = pl.pallas_call(
        kern,
        grid_spec=grid_spec,
        out_shape=jax.ShapeDtypeStruct((b, tp, w), BF16),
        compiler_params=_params("parallel", "arbitrary"),
        name="sb_sample",
    )(page_table, qx, bias_full, g_sb, _later_matrix(PAGE_SIZE), jnp.ones((PAGE_SIZE, PAGE_SIZE), BF16),
      hmask.astype(F32), _head_block_ones(w), new_page(k_new), new_page(v_new),
      *([keys_last(cache_k)] * pps), *([keys_last(cache_v)] * pps))
    return o[:, :t].reshape(b, t, heads, dh)


def _rwkv_prep_kernel(*refs, rw, halo_mode, tiles_per_seq):
    if halo_mode:
        u_ref, halo_ref, st_ref = refs[:3]
        rest = refs[3:]
    else:
        u_ref, up_ref = refs[:2]
        rest = refs[2:]
    (mu_ref, w0_ref, w2h_ref, w2l_ref, a0_ref, a2h_ref, a2l_ref, g2h_ref, g2l_ref,
     kkw_ref, ka_ref, rk_ref, bd_ref,
     r_ref, k_ref, v_ref, lw_ref, kk_ref, b_ref, gate_ref, bonus_ref) = rest
    u = u_ref[...]
    if halo_mode:
        tm = u.shape[0]
        row = lax.broadcasted_iota(jnp.int32, (tm, 1), 0)
        first = (pl.program_id(0) % tiles_per_seq) == 0
        prev_row = jnp.where(first, st_ref[0], halo_ref[7:8, :])
        up = jnp.where(row == 0, prev_row, pltpu.roll(u, 1, axis=0))
    else:
        up = up_ref[...]
    us = u + (up - u) * mu_ref[...]
    r = us[:, 0:rw]
    k = us[:, rw:2 * rw]
    v = us[:, 2 * rw:3 * rw]
    o = 3 * rw
    w_lo = us[:, o:o + LANES]
    a_lo = us[:, o + LANES:o + 2 * LANES]
    g_lo = us[:, o + 2 * LANES:o + 4 * LANES]
    d = w0_ref[...] + _dot3(jnp.tanh(w_lo), w2h_ref[...], w2l_ref[...])
    w_log = -_softplus(-d) - 0.5
    lw_ref[...] = -jnp.exp(w_log)
    a = _sigmoid(a0_ref[...] + _dot3(a_lo, a2h_ref[...], a2l_ref[...]))
    gate_ref[...] = _dot3(_sigmoid(g_lo), g2h_ref[...], g2l_ref[...])
    kkr = k * kkw_ref[...]
    nrm = jnp.sqrt(_dot_exact_rhs(kkr * kkr, bd_ref[...]))
    kk = kkr / jnp.maximum(nrm, KK_EPS)
    k_eff = k * (1.0 + (a - 1.0) * ka_ref[...])
    r_ref[...] = r
    k_ref[...] = k_eff
    v_ref[...] = v
    kk_ref[...] = kk
    b_ref[...] = kk * a
    bonus_ref[...] = _dot_exact_rhs(r * k_eff * rk_ref[...], bd_ref[...]) * v


def rwkv_prep(u, prev, pw, rw, seq_len, *, tm, halo_mode):
    n, uw = u.shape
    tm = min(tm, n)
    c = lambda i: (0, 0)
    row_spec = pl.BlockSpec((tm, uw), lambda i: (i, 0))
    if halo_mode:
        tps = seq_len // tm
        halo_blk = tm // 8
        lead_specs = [row_spec,
                      pl.BlockSpec((8, uw), lambda i: (jnp.maximum(i * halo_blk - 1, 0), 0)),
                      pl.BlockSpec((1, 1, uw), lambda i: (i // tps, 0, 0))]
        lead = (u, u, prev)
    else:
        tps = 1
        lead_specs = [row_spec, row_spec]
        lead = (u, prev)
    vec = lambda width: pl.BlockSpec((1, width), c)
    mat = lambda rows: pl.BlockSpec((rows, rw), c)
    in_specs = lead_specs + [
        vec(uw), vec(rw), mat(LANES), mat(LANES), vec(rw), mat(LANES), mat(LANES),
        mat(2 * LANES), mat(2 * LANES), vec(rw), vec(rw), vec(rw), pl.BlockSpec((rw, rw), c)]
    out_spec = pl.BlockSpec((tm, rw), lambda i: (i, 0))
    kern = functools.partial(_rwkv_prep_kernel, rw=rw, halo_mode=halo_mode, tiles_per_seq=tps)
    return pl.pallas_call(
        kern,
        grid=(n // tm,),
        in_specs=in_specs,
        out_specs=[out_spec] * 8,
        out_shape=[jax.ShapeDtypeStruct((n, rw), F32)] * 8,
        compiler_params=_params("parallel"),
        name="rwkv_prep",
    )(*lead, pw['mu'], pw['w0'], pw['w2h'], pw['w2l'], pw['a0'], pw['a2h'], pw['a2l'],
      pw['g2h'], pw['g2l'], pw['kk'], pw['ka'], pw['rk'], pw['bd'])


def _rwkv_chunk_kernel(r_ref, k_ref, v_ref, lw_ref, kk_ref, b_ref, y_ref, s_out_ref, s_ref, *, chunk, npairs):
    c = pl.program_id(1)
    C = chunk
    C2 = HEADS_PER_TILE * C

    @pl.when(c == 0)
    def _():
        s_ref[...] = jnp.zeros_like(s_ref)

    ci = lax.broadcasted_iota(jnp.int32, (C, C), 0)
    cj = lax.broadcasted_iota(jnp.int32, (C, C), 1)
    ti = lax.broadcasted_iota(jnp.int32, (C2, C2), 0)
    tj = lax.broadcasted_iota(jnp.int32, (C2, C2), 1)
    same_blk = (ti // C) == (tj // C)
    incl = same_blk & (ti >= tj)
    strict = same_blk & (ti > tj)
    eye = (ti == tj).astype(F32)
    lane_head = lax.broadcasted_iota(jnp.int32, (C2, LANES), 1) // HEAD_DIM
    row_head = lax.broadcasted_iota(jnp.int32, (C2, LANES), 0) // C
    own_lanes = lane_head == row_head

    def stack(x):
        return jnp.where(own_lanes, jnp.concatenate([x] * HEADS_PER_TILE, axis=0), jnp.zeros((), BF16))

    lw = lw_ref[...]
    l_hi, l_mid, l_lo = _split3(lw)
    tri = (ci >= cj).astype(BF16)
    cl = _dot(tri, l_hi) + (_dot(tri, l_mid) + _dot(tri, l_lo))
    cl_end = cl[C - 1:C, :]
    e_pos = jnp.exp(cl)
    e_neg = jnp.exp(-cl)
    e_prev = jnp.exp(cl - lw)
    e_end = jnp.exp(cl_end - cl)
    all_ones = jnp.ones((C, LANES), BF16)
    d_rows = jnp.exp(_dot_tn(l_hi, all_ones) + (_dot_tn(l_mid, all_ones) + _dot_tn(l_lo, all_ones)))
    r = r_ref[...]
    k = k_ref[...]
    b = b_ref[...]
    rt = (r * e_pos).astype(BF16)
    kt = (kk_ref[...] * e_prev).astype(BF16)
    k_inv = (k * e_neg).astype(BF16)
    b_inv = (b * e_neg).astype(BF16)
    k_dec = (k * e_end).astype(BF16)
    b_dec = (b * e_end).astype(BF16)
    vb = v_ref[...].astype(BF16)

    tiles = range(npairs)
    sls = [slice(p * LANES, (p + 1) * LANES) for p in tiles]
    st = [s_ref[p] for p in tiles]
    stb = [s.astype(BF16) for s in st]
    v2 = [stack(vb[:, sl]) for sl in sls]
    kt2 = [stack(kt[:, sl]) for sl in sls]
    rt2 = [stack(rt[:, sl]) for sl in sls]
    ab = [_dot_nt(jnp.concatenate([kt2[p], rt2[p]], axis=0),
                  jnp.concatenate([stack(k_inv[:, sls[p]]), stack(b_inv[:, sls[p]])], axis=0))
          for p in tiles]
    a_m = [jnp.where(strict, m[:C2, :C2], 0.0).astype(BF16) for m in ab]
    n_m = [jnp.where(strict, -m[:C2, C2:], 0.0) for m in ab]
    aq_m = [jnp.where(incl, m[C2:, :C2], 0.0).astype(BF16) for m in ab]
    nbq_m = [jnp.where(incl, -m[C2:, C2:], 0.0).astype(BF16) for m in ab]
    g = [eye + m for m in n_m]
    pw = [_dot(m.astype(BF16), m.astype(BF16)) for m in n_m]
    levels = int(math.log2(C))
    for lvl in range(1, levels):
        if lvl < levels - 1:
            gp = [_dot(jnp.concatenate([g[p], pw[p]], axis=0).astype(BF16), pw[p].astype(BF16)) for p in tiles]
            g = [g[p] + gp[p][:C2] for p in tiles]
            pw = [m[C2:] for m in gp]
        else:
            g = [g[p] + _dot(g[p].astype(BF16), pw[p].astype(BF16)) for p in tiles]
    x = [_dot(jnp.concatenate([a_m[p], kt2[p]], axis=1), jnp.concatenate([v2[p], stb[p]], axis=0)) for p in tiles]
    u2b = [_dot(g[p].astype(BF16), x[p].astype(BF16)).astype(BF16) for p in tiles]
    y2 = [_dot(jnp.concatenate([aq_m[p], nbq_m[p], rt2[p]], axis=1),
               jnp.concatenate([v2[p], u2b[p], stb[p]], axis=0)) for p in tiles]
    upd = [_dot_tn(jnp.concatenate([stack(k_dec[:, sls[p]]), stack(b_dec[:, sls[p]])], axis=0),
                   jnp.concatenate([v2[p], -u2b[p]], axis=0)) for p in tiles]
    for p in tiles:
        y = y2[p][:C]
        for hh in range(1, HEADS_PER_TILE):
            y = y + y2[p][hh * C:(hh + 1) * C]
        y_ref[:, sls[p]] = y
        s_new = st[p] * d_rows[sls[p], :] + upd[p]
        s_ref[p] = s_new
        s_out_ref[0, p] = s_new


def rwkv_chunk(r, k, v, lw, kk, b, batch, seq_len):
    n, rw = r.shape
    npairs = rw // LANES
    C = RW_CHUNK
    nc = seq_len // C
    blk = pl.BlockSpec((C, rw), lambda i, c: (i * nc + c, 0))
    kern = functools.partial(_rwkv_chunk_kernel, chunk=C, npairs=npairs)
    return pl.pallas_call(
        kern,
        grid=(batch, nc),
        in_specs=[blk] * 6,
        out_specs=[blk, pl.BlockSpec((1, npairs, LANES, LANES), lambda i, c: (i, 0, 0, 0))],
        out_shape=[jax.ShapeDtypeStruct((n, rw), F32),
                   jax.ShapeDtypeStruct((batch, npairs, LANES, LANES), F32)],
        scratch_shapes=[pltpu.VMEM((npairs, LANES, LANES), F32)],
        compiler_params=_params("parallel", "arbitrary"),
        name="rwkv_chunk",
    )(r, k, v, lw, kk, b)


def _rwkv_seq_kernel(s0_ref, r_ref, k_ref, lw_ref, kk_ref, b_ref, v_ref, y_ref, s_out_ref, *, steps):
    s = s0_ref[0]
    dh = s.shape[-1]
    eye = (lax.broadcasted_iota(jnp.int32, (dh, dh), 0) == lax.broadcasted_iota(jnp.int32, (dh, dh), 1)).astype(F32)
    for t in range(steps):
        vcol = jnp.sum(v_ref[0, t] * eye, axis=-1, keepdims=True)
        sa = jnp.sum(s * kk_ref[0, t], axis=-1, keepdims=True)
        s = s * jnp.exp(lw_ref[0, t]) - sa * b_ref[0, t] + vcol * k_ref[0, t]
        ycol = jnp.sum(s * r_ref[0, t], axis=-1, keepdims=True)
        y_ref[0, t] = jnp.sum(ycol * eye, axis=-2, keepdims=True)
    s_out_ref[0] = s


def rwkv_seq(s0, r, k, v, lw, kk, b):
    bsz, heads, dh, _ = s0.shape
    t = r.shape[1]
    rowv = lambda x: x.reshape(bsz, t, heads, 1, dh)
    row_spec = pl.BlockSpec((1, t, heads, 1, dh), lambda i: (i, 0, 0, 0, 0))
    st_spec = pl.BlockSpec((1, heads, dh, dh), lambda i: (i, 0, 0, 0))
    kern = functools.partial(_rwkv_seq_kernel, steps=t)
    y, s = pl.pallas_call(
        kern,
        grid=(bsz,),
        in_specs=[st_spec] + [row_spec] * 6,
        out_specs=[row_spec, st_spec],
        out_shape=[jax.ShapeDtypeStruct((bsz, t, heads, 1, dh), F32),
                   jax.ShapeDtypeStruct(s0.shape, F32)],
        compiler_params=_params("parallel"),
        name="rwkv_seq",
    )(s0, rowv(r), rowv(k), rowv(lw), rowv(kk), rowv(b), rowv(v))
    return y.reshape(bsz, t, heads * dh), s


def _rwkv_epi_kernel(y_ref, bonus_ref, gate_ref, g_ref, b_ref, bd_ref, o_ref):
    y = y_ref[...]
    inv = 1.0 / HEAD_DIM
    mu = _dot_exact_rhs(y, bd_ref[...]) * inv
    yc = y - mu
    var = _dot_exact_rhs(yc * yc, bd_ref[...]) * inv
    yn = yc * lax.rsqrt(var + LNX_EPS) * g_ref[...] + b_ref[...]
    o_ref[...] = ((yn + bonus_ref[...]) * gate_ref[...]).astype(BF16)


def rwkv_epilogue(y, bonus, gate, ln_g, ln_b, bd, *, tm):
    n, rw = y.shape
    tm = min(tm, n)
    blk = pl.BlockSpec((tm, rw), lambda i: (i, 0))
    vec = pl.BlockSpec((1, rw), lambda i: (0, 0))
    return pl.pallas_call(
        _rwkv_epi_kernel,
        grid=(n // tm,),
        in_specs=[blk, blk, blk, vec, vec, pl.BlockSpec((rw, rw), lambda i: (0, 0))],
        out_specs=blk,
        out_shape=jax.ShapeDtypeStruct((n, rw), BF16),
        compiler_params=_params("parallel"),
        name="rwkv_epilogue",
    )(y, bonus, gate, ln_g, ln_b, bd)


def _pad_rows(m, rows):
    return jnp.pad(m, ((0, rows - m.shape[0]), (0, 0)))


def _hi_lo(m):
    hi = m.astype(BF16)
    return hi, (m - hi.astype(F32)).astype(BF16)


def _u_layout(rw, n_decay, n_icl, n_gate):
    segs = [(3 * rw, 3 * rw), (n_decay, LANES), (n_icl, LANES), (n_gate, 2 * LANES)]
    assert n_decay <= LANES and n_icl <= LANES and n_gate <= 2 * LANES
    return segs


def _pad_u_cols(m, segs):
    out, o = [], 0
    for width, padded in segs:
        out.append(m[..., o:o + width])
        if padded > width:
            out.append(jnp.zeros(m.shape[:-1] + (padded - width,), m.dtype))
        o += width
    return jnp.concatenate(out, axis=-1)


def _unpad_u_cols(m, segs):
    out, o = [], 0
    for width, padded in segs:
        out.append(m[..., o:o + width])
        o += padded
    return jnp.concatenate(out, axis=-1)


def _prepare_weights(w_in, w_out, g_pre_mix, g_post_mix, g_pre_ffn, g_post_ffn, g_sb_out, sb_bias,
                     rw_mu, rw_w0, rw_w2, rw_a0, rw_a2, rw_g2, rw_kk, rw_ka, rw_rk, rw_ln_g, rw_ln_b,
                     w_up, conv_w, conv_b, w_down):
    rw = rw_w0.shape[0]
    sbw = g_sb_out.shape[0]
    segs = _u_layout(rw, rw_w2.shape[0], rw_a2.shape[0], rw_g2.shape[0])
    row = lambda x: x.reshape(1, -1)
    w2h, w2l = _hi_lo(_pad_rows(rw_w2, LANES))
    a2h, a2l = _hi_lo(_pad_rows(rw_a2, LANES))
    g2h, g2l = _hi_lo(_pad_rows(rw_g2, 2 * LANES))
    bd = _head_block_ones(rw)
    w_in_p = jnp.concatenate([w_in[:, :3 * sbw], _pad_u_cols(w_in[:, 3 * sbw:], segs)], axis=1).astype(BF16)
    prep = dict(mu=row(_pad_u_cols(rw_mu, segs)), w0=row(rw_w0), w2h=w2h, w2l=w2l, a0=row(rw_a0),
                a2h=a2h, a2l=a2l, g2h=g2h, g2l=g2l, kk=row(rw_kk), ka=row(rw_ka),
                rk=row(rw_rk), bd=bd)
    return dict(
        segs=segs, rw=rw, sbw=sbw, w_in=w_in_p, w_out=w_out.astype(BF16),
        g_pre_mix=row(g_pre_mix), g_post_mix=row(g_post_mix), g_pre_ffn=row(g_pre_ffn),
        g_post_ffn=row(g_post_ffn), g_sb=row(g_sb_out), sb_bias=sb_bias, prep=prep,
        ln_g=row(rw_ln_g), ln_b=row(rw_ln_b), bd=bd,
        w_up=w_up.astype(BF16), conv_w=conv_w, conv_b=row(conv_b), w_down=w_down.astype(BF16))


def _tail(wts, x, o_sb, o_rw, *, tm):
    assert o_sb.shape == o_rw.shape
    return mm_norm_resid((o_sb, o_rw), wts['w_out'], x, wts['g_post_mix'], wts['g_pre_ffn'],
                         tm=tm, tk=o_sb.shape[1], emit_h=True)


def _down(wts, act, x1, *, tm):
    (y,) = mm_norm_resid(act, wts['w_down'], x1, wts['g_post_ffn'], wts['g_post_ffn'],
                         tm=tm, tk=act.shape[1] // 4, emit_h=False)
    return y


def _prompt_layer(wts, x3):
    bsz, t, d = x3.shape
    n = bsz * t
    x = x3.reshape(n, d)
    rw, sbw, segs = wts['rw'], wts['sbw'], wts['segs']
    uw = sum(p for _, p in segs)
    q, k, v, u = in_proj(x, wts['g_pre_mix'], wts['w_in'], sbw, uw, tm=1024)
    o_sb = sb_prompt(q, k, v, wts['sb_bias'], wts['g_sb'], bsz, t, tq=min(512, t), tk=min(256, t))
    shift0 = jnp.zeros((bsz, 1, uw), F32)
    r, ke, vr, lw, kk, b, gate, bonus = rwkv_prep(u, shift0, wts['prep'], rw, t, tm=min(256, t), halo_mode=True)
    y, s_pairs = rwkv_chunk(r, ke, vr, lw, kk, b, bsz, t)
    o_rw = rwkv_epilogue(y, bonus, gate, wts['ln_g'], wts['ln_b'], wts['bd'], tm=512)
    x1, h2 = _tail(wts, x, o_sb, o_rw, tm=512)
    tm_f = min(1024, t)
    conv0 = jnp.zeros((bsz, CONV_W - 1, wts['conv_w'].shape[1]), F32)
    act, gtail = ffn_up_prompt(h2, wts['w_up'], wts['conv_w'], wts['conv_b'], conv0, t, tm=tm_f)
    yout = _down(wts, act, x1, tm=512)
    heads = rw // HEAD_DIM
    hp = HEADS_PER_TILE
    sp = s_pairs.reshape(bsz, rw // LANES, hp, HEAD_DIM, hp, HEAD_DIM)
    wkv = jnp.stack([jnp.swapaxes(sp[:, :, i, :, i, :], -1, -2) for i in range(hp)], axis=2)
    wkv = wkv.reshape(bsz, heads, HEAD_DIM, HEAD_DIM)
    shift = _unpad_u_cols(u.reshape(bsz, t, uw)[:, -1], segs)
    tps = t // tm_f
    conv_new = gtail.reshape(bsz, tps, 8, -1)[:, -1, 8 - (CONV_W - 1):]
    kvshape = (bsz, t, sbw // HEAD_DIM, HEAD_DIM)
    return yout.reshape(bsz, t, d), k.reshape(kvshape), v.reshape(kvshape), wkv, shift, conv_new


def _sample_layer(wts, x3, cache_k, cache_v, layer, page_table, wkv0, shift_prev, conv_prev):
    bsz, t, d = x3.shape
    n = bsz * t
    x = x3.reshape(n, d)
    rw, sbw, segs = wts['rw'], wts['sbw'], wts['segs']
    uw = sum(p for _, p in segs)
    q, k, v, u = in_proj(x, wts['g_pre_mix'], wts['w_in'], sbw, uw, tm=n)
    kvshape = (bsz, t, sbw // HEAD_DIM, HEAD_DIM)
    o_sb = sb_sample(q.reshape(kvshape), k.reshape(kvshape), v.reshape(kvshape),
                     wts['sb_bias'], wts['g_sb'], cache_k, cache_v, layer, page_table).reshape(n, sbw)
    u3 = u.reshape(bsz, t, uw)
    u_prev = jnp.concatenate([_pad_u_cols(shift_prev, segs)[:, None], u3[:, :-1]], axis=1).reshape(n, uw)
    r, ke, vr, lw, kk, b, gate, bonus = rwkv_prep(u, u_prev, wts['prep'], rw, t, tm=n, halo_mode=False)
    as3 = lambda a: a.reshape(bsz, t, rw)
    y3, wkv = rwkv_seq(wkv0, as3(r), as3(ke), as3(vr), as3(lw), as3(kk), as3(b))
    o_rw = rwkv_epilogue(y3.reshape(n, rw), bonus, gate, wts['ln_g'], wts['ln_b'], wts['bd'], tm=n)
    x1, h2 = _tail(wts, x, o_sb, o_rw, tm=n)
    f = wts['conv_w'].shape[1]
    zeros = jnp.zeros((bsz, t, f), F32)
    p1 = zeros.at[:, 0].set(conv_prev[:, 1]).reshape(n, f)
    p2 = zeros.at[:, 0].set(conv_prev[:, 0]).at[:, 1].set(conv_prev[:, 1]).reshape(n, f)
    act, gfull = ffn_up_sample(h2, wts['w_up'], wts['conv_w'], wts['conv_b'], p1, p2, t)
    yout = _down(wts, act, x1, tm=n)
    shift = _unpad_u_cols(u3[:, -1], segs)
    conv_new = gfull.reshape(bsz, t, f)[:, t - (CONV_W - 1):]
    kvshape = (bsz, t, sbw // HEAD_DIM, HEAD_DIM)
    return yout.reshape(bsz, t, d), k.reshape(kvshape), v.reshape(kvshape), wkv, shift, conv_new


def kernel(x_prompt, x_sample, cache_sb_k, cache_sb_v, state_rwkv_wkv, state_rwkv_shift, state_ffn_conv, page_table, w_in, w_out, g_pre_mix, g_post_mix, g_pre_ffn, g_post_ffn, g_sb_out, sb_bias, rw_mu, rw_w0, rw_w2, rw_a0, rw_a2, rw_g2, rw_kk, rw_ka, rw_rk, rw_ln_g, rw_ln_b, w_up, conv_w, conv_b, w_down):
    depth = w_in.shape[0]
    yp, ys = x_prompt, x_sample
    outs_p, outs_s = [], []
    for l in range(depth):
        wts = _prepare_weights(w_in[l], w_out[l], g_pre_mix[l], g_post_mix[l], g_pre_ffn[l], g_post_ffn[l],
                               g_sb_out[l], sb_bias[l], rw_mu[l], rw_w0[l], rw_w2[l], rw_a0[l], rw_a2[l],
                               rw_g2[l], rw_kk[l], rw_ka[l], rw_rk[l].reshape(-1), rw_ln_g[l], rw_ln_b[l],
                               w_up[l], conv_w[l], conv_b[l], w_down[l])
        yp, *rest_p = _prompt_layer(wts, yp)
        ys, *rest_s = _sample_layer(wts, ys, cache_sb_k, cache_sb_v, l, page_table,
                                    state_rwkv_wkv[l], state_rwkv_shift[l], state_ffn_conv[l])
        outs_p.append(rest_p)
        outs_s.append(rest_s)
    stack = lambda outs, i: jnp.stack([o[i] for o in outs])
    return (yp, ys,
            stack(outs_p, 0), stack(outs_p, 1), stack(outs_p, 2), stack(outs_p, 3), stack(outs_p, 4),
            stack(outs_s, 0), stack(outs_s, 1), stack(outs_s, 2), stack(outs_s, 3), stack(outs_s, 4))
```

```python
import functools
import math

import jax
import jax.numpy as jnp
from jax import lax
from jax.experimental import pallas as pl
from jax.experimental.pallas import tpu as pltpu

F32 = jnp.float32
BF16 = jnp.bfloat16

HEAD_DIM = 64
LANES = 128
HEADS_PER_TILE = LANES // HEAD_DIM
NORM_EPS = 1e-6
LNX_EPS = 64e-5
KK_EPS = 1e-12
CONV_W = 3
PAGE_SIZE = 128
VMEM_LIMIT = 48 * 1024 * 1024
RW_CHUNK = 64
HEAD_SUM_TILE = 256


def _dot(a, b):
    return jnp.dot(a, b, preferred_element_type=F32)


def _dot_nt(a, b):
    return lax.dot_general(a, b, (((1,), (1,)), ((), ())), preferred_element_type=F32)


def _dot_tn(a, b):
    return lax.dot_general(a, b, (((0,), (0,)), ((), ())), preferred_element_type=F32)


def _split2(x):
    hi = x.astype(BF16)
    lo = (x - hi.astype(F32)).astype(BF16)
    return hi, lo


def _split3(x):
    hi = x.astype(BF16)
    r1 = x - hi.astype(F32)
    mid = r1.astype(BF16)
    lo = (r1 - mid.astype(F32)).astype(BF16)
    return hi, mid, lo


def _dot_exact_rhs(x, m01):
    hi, lo = _split2(x)
    t = m01.shape[0]
    cols = [slice(c * t, (c + 1) * t) for c in range(x.shape[1] // t)]
    his = [_dot(hi[:, c], m01) for c in cols]
    los = [_dot(lo[:, c], m01) for c in cols]
    out = [h + l for h, l in zip(his, los)]
    return out[0] if len(out) == 1 else jnp.concatenate(out, axis=1)


def _dot3(a, b_hi, b_lo):
    a_hi, a_lo = _split2(a)
    return _dot(a_hi, b_hi) + (_dot(a_lo, b_hi) + _dot(a_hi, b_lo))


def _softplus(x):
    return jnp.maximum(x, 0.0) + jnp.log(1.0 + jnp.exp2(jnp.abs(x) * (-math.log2(math.e))))


def _sigmoid(x):
    return 1.0 / (1.0 + jnp.exp(-x))


def _rms_rows(x, g):
    return x * lax.rsqrt(jnp.mean(x * x, axis=-1, keepdims=True) + NORM_EPS) * g


def _params(*sem):
    return pltpu.CompilerParams(dimension_semantics=sem, vmem_limit_bytes=VMEM_LIMIT)


def _head_block_ones(n):
    i = jnp.arange(n) // HEAD_DIM
    return (i[:, None] == i[None, :]).astype(BF16)


def _in_proj_kernel(x_ref, g_ref, w_ref, *rest, nq, nk, nv, kv_t):
    if kv_t:
        wt_ref, q_ref, k_ref, v_ref, u_ref, h_ref = rest
    else:
        q_ref, k_ref, v_ref, u_ref, h_ref = rest
    j = pl.program_id(1)

    @pl.when(j == 0)
    def _():
        h_ref[...] = _rms_rows(x_ref[...], g_ref[...]).astype(BF16)

    def kv_block():
        return _dot_nt(wt_ref[...], h_ref[...]) if kv_t else _dot(h_ref[...], w_ref[...])

    @pl.when(j < nq)
    def _():
        q_ref[...] = (_dot(h_ref[...], w_ref[...]) * (HEAD_DIM ** -0.5)).astype(BF16)

    @pl.when((j >= nq) & (j < nq + nk))
    def _():
        k_ref[...] = kv_block()

    @pl.when((j >= nq + nk) & (j < nq + nk + nv))
    def _():
        v_ref[...] = kv_block()

    @pl.when(j >= nq + nk + nv)
    def _():
        u_ref[...] = _dot(h_ref[...], w_ref[...])


def in_proj(x, g, w, sb_width, u_width, *, tm, tn=512, w_kv_t=None, seq_len=None):
    n, d = x.shape
    tm = min(tm, n)
    nq = nk = nv = sb_width // tn
    nu = u_width // tn
    nj = nq + nk + nv + nu
    off_k, off_v, off_u = nq, nq + nk, nq + nk + nv
    clip = lambda j, lo, cnt: jnp.clip(j - lo, 0, cnt - 1)
    kv_t = w_kv_t is not None
    kern = functools.partial(_in_proj_kernel, nq=nq, nk=nk, nv=nv, kv_t=kv_t)
    in_specs = [
        pl.BlockSpec((tm, d), lambda i, j: (i, 0)),
        pl.BlockSpec((1, d), lambda i, j: (0, 0)),
        pl.BlockSpec((d, tn), lambda i, j: (0, j)),
    ]
    operands = (x, g, w)
    if kv_t:
        tps = seq_len // tm
        in_specs.append(pl.BlockSpec((tn, d), lambda i, j: (clip(j, off_k, nk + nv), 0)))
        operands += (w_kv_t,)
        kv_specs = [pl.BlockSpec((None, tn, tm), lambda i, j: (i // tps, clip(j, off_k, nk), i % tps)),
                    pl.BlockSpec((None, tn, tm), lambda i, j: (i // tps, clip(j, off_v, nv), i % tps))]
        kv_shape = jax.ShapeDtypeStruct((n // seq_len, sb_width, seq_len), F32)
    else:
        kv_specs = [pl.BlockSpec((tm, tn), lambda i, j: (i, clip(j, off_k, nk))),
                    pl.BlockSpec((tm, tn), lambda i, j: (i, clip(j, off_v, nv)))]
        kv_shape = jax.ShapeDtypeStruct((n, sb_width), F32)
    return pl.pallas_call(
        kern,
        grid=(n // tm, nj),
        in_specs=in_specs,
        out_specs=[pl.BlockSpec((tm, tn), lambda i, j: (i, clip(j, 0, nq)))] + kv_specs
                  + [pl.BlockSpec((tm, tn), lambda i, j: (i, clip(j, off_u, nu)))],
        out_shape=[jax.ShapeDtypeStruct((n, sb_width), BF16), kv_shape, kv_shape,
                   jax.ShapeDtypeStruct((n, u_width), F32)],
        scratch_shapes=[pltpu.VMEM((tm, d), BF16)],
        compiler_params=_params("parallel", "arbitrary"),
        name="in_proj",
    )(*operands)


def _mm_norm_resid_kernel(a_ref, w_ref, r_ref, g_ref, g2_ref, *rest, nk, emit_h):
    if emit_h:
        o_ref, h_ref, acc_ref = rest
    else:
        o_ref, acc_ref = rest
        h_ref = None
    kk = pl.program_id(1)

    def finalize(m):
        y = r_ref[...] + _rms_rows(m, g_ref[...])
        o_ref[...] = y
        if emit_h:
            h_ref[...] = _rms_rows(y, g2_ref[...]).astype(BF16)

    def k_step(j):
        part = _dot(a_ref[...], w_ref[...])
        if nk == 1:
            finalize(part)
        elif j == 0:
            acc_ref[...] = part
        elif j < nk - 1:
            acc_ref[...] += part
        else:
            finalize(acc_ref[...] + part)

    if nk == 1:
        k_step(0)
    else:
        for j in range(nk):
            pl.when(kk == j)(functools.partial(k_step, j))


def mm_norm_resid(a, w, resid, g, g2, *, tm, tk, emit_h):
    n, kdim = a.shape
    d = w.shape[1]
    tm = min(tm, n)
    nk = kdim // tk
    kern = functools.partial(_mm_norm_resid_kernel, nk=nk, emit_h=emit_h)
    out_specs = [pl.BlockSpec((tm, d), lambda i, k: (i, 0))]
    out_shape = [jax.ShapeDtypeStruct((n, d), F32)]
    if emit_h:
        out_specs.append(pl.BlockSpec((tm, d), lambda i, k: (i, 0)))
        out_shape.append(jax.ShapeDtypeStruct((n, d), BF16))
    return pl.pallas_call(
        kern,
        grid=(n // tm, nk),
        in_specs=[
            pl.BlockSpec((tm, tk), lambda i, k: (i, k)),
            pl.BlockSpec((tk, d), lambda i, k: (k, 0)),
            pl.BlockSpec((tm, d), lambda i, k: (i, 0)),
            pl.BlockSpec((1, d), lambda i, k: (0, 0)),
            pl.BlockSpec((1, d), lambda i, k: (0, 0)),
        ],
        out_specs=out_specs,
        out_shape=out_shape,
        scratch_shapes=[pltpu.VMEM((tm, d), F32)],
        compiler_params=_params("parallel", "arbitrary"),
        name="mm_norm_resid",
    )(a, w, resid, g, g2)


def _gelu_tanh(x):
    c = math.sqrt(2.0 / math.pi)
    return x * (0.5 * (1.0 + jnp.tanh(c * (x + 0.044715 * (x * x * x)))))


def _ffn_up_kernel(*refs, tiles_per_seq, seq_len, halo_mode):
    if halo_mode:
        h_ref, halo_ref, wg_ref, wv_ref, cw_ref, cb_ref, prev_ref, act_ref, g_out_ref = refs
    else:
        h_ref, wg_ref, wv_ref, cw_ref, cb_ref, p1_ref, p2_ref, act_ref, g_out_ref = refs
    h = h_ref[...]
    g = _dot(h, wg_ref[...])
    val = _dot(h, wv_ref[...])
    tm = g.shape[0]
    row = lax.broadcasted_iota(jnp.int32, (tm, 1), 0)
    r1 = pltpu.roll(g, 1, axis=0)
    r2 = pltpu.roll(g, 2, axis=0)
    if halo_mode:
        gh = _dot(halo_ref[...], wg_ref[...])
        first = (pl.program_id(0) % tiles_per_seq) == 0
        pm1 = jnp.where(first, prev_ref[0, 1:2, :], gh[7:8, :])
        pm2 = jnp.where(first, prev_ref[0, 0:1, :], gh[6:7, :])
        g1 = jnp.where(row == 0, pm1, r1)
        g2 = jnp.where(row == 0, pm2, jnp.where(row == 1, pm1, r2))
        g_out_ref[0] = g[tm - 8:, :]
    else:
        t = row % seq_len
        g1 = jnp.where(t >= 1, r1, p1_ref[...])
        g2 = jnp.where(t >= 2, r2, p2_ref[...])
        g_out_ref[...] = g
    conv = cb_ref[...] + cw_ref[0:1, :] * g2 + cw_ref[1:2, :] * g1 + cw_ref[2:3, :] * g
    act_ref[...] = (_gelu_tanh(conv) * val).astype(BF16)


def ffn_up_prompt(h, w_up, conv_w, conv_b, conv_prev, seq_len, *, tm, tn=512):
    n, d = h.shape
    f = w_up.shape[1] // 2
    nf = f // tn
    tps = seq_len // tm
    kern = functools.partial(_ffn_up_kernel, tiles_per_seq=tps, seq_len=seq_len, halo_mode=True)
    halo_blk = tm // 8
    return pl.pallas_call(
        kern,
        grid=(n // tm, nf),
        in_specs=[
            pl.BlockSpec((tm, d), lambda i, j: (i, 0)),
            pl.BlockSpec((8, d), lambda i, j: (jnp.maximum(i * halo_blk - 1, 0), 0)),
            pl.BlockSpec((d, tn), lambda i, j: (0, j)),
            pl.BlockSpec((d, tn), lambda i, j: (0, j + nf)),
            pl.BlockSpec((CONV_W, tn), lambda i, j: (0, j)),
            pl.BlockSpec((1, tn), lambda i, j: (0, j)),
            pl.BlockSpec((1, CONV_W - 1, tn), lambda i, j: (i // tps, 0, j)),
        ],
        out_specs=[
            pl.BlockSpec((tm, tn), lambda i, j: (i, j)),
            pl.BlockSpec((1, 8, tn), lambda i, j: (i, 0, j)),
        ],
        out_shape=[
            jax.ShapeDtypeStruct((n, f), BF16),
            jax.ShapeDtypeStruct((n // tm, 8, f), F32),
        ],
        compiler_params=_params("parallel", "arbitrary"),
        name="ffn_up_prompt",
    )(h, h, w_up, w_up, conv_w, conv_b, conv_prev)


def ffn_up_sample(h, w_up, conv_w, conv_b, p1, p2, seq_len, *, tn=512):
    n, d = h.shape
    f = w_up.shape[1] // 2
    nf = f // tn
    kern = functools.partial(_ffn_up_kernel, tiles_per_seq=1, seq_len=seq_len, halo_mode=False)
    return pl.pallas_call(
        kern,
        grid=(1, nf),
        in_specs=[
            pl.BlockSpec((n, d), lambda i, j: (0, 0)),
            pl.BlockSpec((d, tn), lambda i, j: (0, j)),
            pl.BlockSpec((d, tn), lambda i, j: (0, j + nf)),
            pl.BlockSpec((CONV_W, tn), lambda i, j: (0, j)),
            pl.BlockSpec((1, tn), lambda i, j: (0, j)),
            pl.BlockSpec((n, tn), lambda i, j: (0, j)),
            pl.BlockSpec((n, tn), lambda i, j: (0, j)),
        ],
        out_specs=[
            pl.BlockSpec((n, tn), lambda i, j: (0, j)),
            pl.BlockSpec((n, tn), lambda i, j: (0, j)),
        ],
        out_shape=[
            jax.ShapeDtypeStruct((n, f), BF16),
            jax.ShapeDtypeStruct((n, f), F32),
        ],
        compiler_params=_params("arbitrary", "arbitrary"),
        name="ffn_up_sample",
    )(h, w_up, w_up, conv_w, conv_b, p1, p2)


def _sb_prompt_kernel(bias_ref, q_ref, k_ref, v_ref, later_ref, bd_ref, g_ref, o_ref,
                      kb_ref, vb_ref, carry_ref, acc_ref, *, tq, tk):
    p = pl.program_id(1)
    qi = pl.program_id(2)
    ratio = tq // tk

    @pl.when(qi == 0)
    def _():
        kb_ref[...] = k_ref[...].astype(BF16)
        vb_ref[...] = v_ref[...].astype(BF16)

    q = q_ref[...]
    lane = lax.broadcasted_iota(jnp.int32, (1, LANES), 1)
    qms = [jnp.where((lane // HEAD_DIM) == hh, q, jnp.zeros_like(q)) for hh in range(HEADS_PER_TILE)]
    biases = [bias_ref[p * HEADS_PER_TILE + hh] for hh in range(HEADS_PER_TILE)]
    carry_ref[...] = jnp.zeros_like(carry_ref)
    acc_ref[...] = jnp.zeros_like(acc_ref)
    later = later_ref[...]

    hs = range(HEADS_PER_TILE)

    def key_tiles(jq, masked):
        order = range(ratio - 1, -1, -1)
        j0 = [pl.multiple_of((jq * ratio + d) * tk, tk) for d in order]
        n = len(j0)
        kblk = [kb_ref[:, pl.ds(s, tk)] for s in j0]
        vblk = [vb_ref[:, pl.ds(s, tk)] for s in j0]
        mask = [None] * n
        if masked:
            qpos = qi * tq + lax.broadcasted_iota(jnp.int32, (tq, tk), 0)
            kcol = lax.broadcasted_iota(jnp.int32, (tq, tk), 1)
            mask = [s + kcol < qpos for s in j0]
        z = [[_dot(qms[hh], kblk[i]) + biases[hh] for hh in hs] for i in range(n)]
        sp = [[_softplus(z[i][hh]) for hh in hs] for i in range(n)]
        if masked:
            sp = [[jnp.where(mask[i], s, 0.0) for s in sp[i]] for i in range(n)]
        ls = [[z[i][hh] - sp[i][hh] for hh in hs] for i in range(n)]
        after = [[_dot(s.astype(BF16), later) for s in sp[i]] for i in range(n)]
        total = [[jnp.sum(s, axis=-1, keepdims=True) for s in sp[i]] for i in range(n)]
        c = [carry_ref[hh] for hh in hs]
        pv = [None] * HEADS_PER_TILE
        for i in range(n):
            for hh in hs:
                w = jnp.exp(ls[i][hh] - after[i][hh] - jnp.concatenate([c[hh]] * (tk // LANES), axis=1))
                if masked:
                    w = jnp.where(mask[i], w, 0.0)
                term = _dot_nt(w.astype(BF16), vblk[i])
                pv[hh] = term if pv[hh] is None else pv[hh] + term
                c[hh] = c[hh] + total[i][hh]
        for hh in hs:
            acc_ref[hh] += pv[hh]
            carry_ref[hh] = c[hh]

    key_tiles(qi, True)

    def body(it, c):
        key_tiles(qi - 1 - it, False)
        return c

    lax.fori_loop(0, qi, body, 0)
    o = acc_ref[0]
    for hh in range(1, HEADS_PER_TILE):
        o = jnp.where((lane // HEAD_DIM) == hh, acc_ref[hh], o)
    ms = _dot_exact_rhs(o * o, bd_ref[...]) * (1.0 / HEAD_DIM)
    o_ref[...] = (o * lax.rsqrt(ms + NORM_EPS) * g_ref[...]).astype(BF16)


def _later_matrix(n):
    idx = jnp.arange(n)
    return (idx[:, None] > idx[None, :]).astype(BF16)


def sb_prompt(q, k, v, bias, g_sb, batch, seq_len, *, tq, tk):
    n, w = q.shape
    npairs = w // LANES
    nq = seq_len // tq
    kern = functools.partial(_sb_prompt_kernel, tq=tq, tk=tk)
    const = lambda b, p, i: (0, 0)
    return pl.pallas_call(
        kern,
        grid=(batch, npairs, nq),
        in_specs=[
            pl.BlockSpec(memory_space=pltpu.SMEM),
            pl.BlockSpec((tq, LANES), lambda b, p, i: (b * nq + i, p)),
            pl.BlockSpec((None, LANES, seq_len), lambda b, p, i: (b, p, 0)),
            pl.BlockSpec((None, LANES, seq_len), lambda b, p, i: (b, p, 0)),
            pl.BlockSpec((tk, tk), const),
            pl.BlockSpec((LANES, LANES), const),
            pl.BlockSpec((1, LANES), lambda b, p, i: (0, p)),
        ],
        out_specs=pl.BlockSpec((tq, LANES), lambda b, p, i: (b * nq + i, p)),
        out_shape=jax.ShapeDtypeStruct((n, w), BF16),
        scratch_shapes=[
            pltpu.VMEM((LANES, seq_len), BF16),
            pltpu.VMEM((LANES, seq_len), BF16),
            pltpu.VMEM((HEADS_PER_TILE, tq, LANES), F32),
            pltpu.VMEM((HEADS_PER_TILE, tq, LANES), F32),
        ],
        compiler_params=_params("parallel", "parallel", "arbitrary"),
        name="sb_prompt",
    )(bias, q, k, v, _later_matrix(tk), _head_block_ones(LANES), g_sb)


SAMPLE_T_PAD = 8
PAGES_PER_STEP = 8


def _sb_sample_kernel(pt_ref, q_ref, bias_ref, g_ref, later_ref, ones_ref, hm_ref, bd_ref, *rest, heads, pps):
    del pt_ref
    kn, vn = rest[0:1], rest[1:2]
    kp, vp = rest[2:2 + pps], rest[2 + pps:2 + 2 * pps]
    o_ref, carry_ref, acc_ref = rest[2 + 2 * pps:]
    pg = pl.program_id(1)
    qx = q_ref[0]
    rows = qx.shape[0]
    bias = bias_ref[...]
    later = later_ref[...]
    ones = ones_ref[...]

    def sweep(k_refs, v_refs, carry, acc, mask):
        n = len(k_refs)
        z = [_dot(qx, k_refs[i][...].astype(BF16)) + bias for i in range(n)]
        sp = [_softplus(x) for x in z]
        if mask is not None:
            sp = [jnp.where(mask, s, 0.0) for s in sp]
        spb = [s.astype(BF16) for s in sp]
        after = [_dot(s, later) for s in spb]
        total = [_dot(s, ones) for s in spb]
        w = []
        for i in range(n):
            wi = jnp.exp(z[i] - sp[i] - after[i] - carry)
            w.append(wi if mask is None else jnp.where(mask, wi, 0.0))
            carry = carry + total[i]
        pv = [_dot_nt(w[i].astype(BF16), v_refs[i][...].astype(BF16)) for i in range(n)]
        for i in range(n):
            acc = acc + pv[i]
        return carry, acc

    def store_state(carry, acc):
        carry_ref[...] = carry
        acc_ref[...] = acc

    @pl.when(pg == 0)
    def _():
        t = lax.broadcasted_iota(jnp.int32, (rows, PAGE_SIZE), 0) // heads
        s = lax.broadcasted_iota(jnp.int32, (rows, PAGE_SIZE), 1)
        store_state(*sweep(kn, vn, jnp.zeros((rows, PAGE_SIZE), F32), jnp.zeros(acc_ref.shape, F32), s < t))

    store_state(*sweep(kp, vp, carry_ref[...], acc_ref[...], None))

    @pl.when(pg == pl.num_programs(1) - 1)
    def _():
        a = acc_ref[...] * hm_ref[...]
        o = jnp.sum(a.reshape(rows // heads, heads, a.shape[1]), axis=1)
        ms = _dot_exact_rhs(o * o, bd_ref[...]) * (1.0 / HEAD_DIM)
        o_ref[0] = (o * lax.rsqrt(ms + NORM_EPS) * g_ref[...]).astype(BF16)


def sb_sample(q, k_new, v_new, bias, g_sb, cache_k, cache_v, layer, page_table):
    b, t, heads, dh = q.shape
    w = heads * dh
    tp = SAMPLE_T_PAD
    pps = PAGES_PER_STEP
    n_pages = page_table.shape[1]
    rows = tp * heads
    keys_last = lambda x: jnp.moveaxis(x, -3, -1).reshape(x.shape[:-3] + (w, x.shape[-3]))
    new_page = lambda x: jnp.pad(keys_last(x), ((0, 0), (0, 0), (0, PAGE_SIZE - t)))
    head_of_col = jnp.arange(w) // dh
    head_of_row = jnp.arange(rows) % heads
    hmask = head_of_row[:, None] == head_of_col[None, :]
    qp = jnp.pad(q.reshape(b, t, w), ((0, 0), (0, tp - t), (0, 0)))
    qx = jnp.where(hmask[None], jnp.repeat(qp, heads, axis=1), jnp.zeros((), BF16))
    bias_full = jnp.broadcast_to(bias[head_of_row][:, None], (rows, PAGE_SIZE)).astype(F32)
    kern = functools.partial(_sb_sample_kernel, heads=heads, pps=pps)
    c2 = lambda i, p, pt: (0, 0)
    per_b = lambda i, p, pt: (i, 0, 0)
    new_spec = pl.BlockSpec((None, w, PAGE_SIZE), per_b)
    page = lambda s: (lambda i, p, pt: (layer, pt[i, n_pages - 1 - (p * pps + s)], 0, 0))
    page_specs = [pl.BlockSpec((None, None, w, PAGE_SIZE), page(s)) for s in range(pps)]
    grid_spec = pltpu.PrefetchScalarGridSpec(
        num_scalar_prefetch=1,
        grid=(b, n_pages // pps),
        in_specs=[
            pl.BlockSpec((1, rows, w), per_b),
            pl.BlockSpec((rows, PAGE_SIZE), c2),
            pl.BlockSpec((1, w), c2),
            pl.BlockSpec((PAGE_SIZE, PAGE_SIZE), c2),
            pl.BlockSpec((PAGE_SIZE, PAGE_SIZE), c2),
            pl.BlockSpec((rows, w), c2),
            pl.BlockSpec((HEAD_SUM_TILE, HEAD_SUM_TILE), c2),
            new_spec, new_spec,
        ] + page_specs + page_specs,
        out_specs=pl.BlockSpec((1, tp, w), per_b),
        scratch_shapes=[
            pltpu.VMEM((rows, PAGE_SIZE), F32),
            pltpu.VMEM((rows, w), F32),
        ],
    )
    o = pl.pallas_call(
        kern,
        grid_spec=grid_spec,
        out_shape=jax.ShapeDtypeStruct((b, tp, w), BF16),
        compiler_params=_params("parallel", "arbitrary"),
        name="sb_sample",
    )(page_table, qx, bias_full, g_sb, _later_matrix(PAGE_SIZE), jnp.ones((PAGE_SIZE, PAGE_SIZE), BF16),
      hmask.astype(F32), _head_block_ones(HEAD_SUM_TILE), new_page(k_new), new_page(v_new),
      *([keys_last(cache_k)] * pps), *([keys_last(cache_v)] * pps))
    return o[:, :t].reshape(b, t, heads, dh)


def _rwkv_prep_kernel(*refs, rw, halo_mode, tiles_per_seq):
    if halo_mode:
        u_ref, halo_ref, st_ref = refs[:3]
        rest = refs[3:]
    else:
        u_ref, up_ref = refs[:2]
        rest = refs[2:]
    (mu_ref, w0_ref, w2h_ref, w2l_ref, a0_ref, a2h_ref, a2l_ref, g2h_ref, g2l_ref,
     kkw_ref, ka_ref, rk_ref, bd_ref,
     r_ref, k_ref, v_ref, lw_ref, kk_ref, b_ref, gate_ref, bonus_ref) = rest
    u = u_ref[...]
    if halo_mode:
        tm = u.shape[0]
        row = lax.broadcasted_iota(jnp.int32, (tm, 1), 0)
        first = (pl.program_id(0) % tiles_per_seq) == 0
        prev_row = jnp.where(first, st_ref[0], halo_ref[7:8, :])
        up = jnp.where(row == 0, prev_row, pltpu.roll(u, 1, axis=0))
    else:
        up = up_ref[...]
    us = u + (up - u) * mu_ref[...]
    r = us[:, 0:rw]
    k = us[:, rw:2 * rw]
    v = us[:, 2 * rw:3 * rw]
    o = 3 * rw
    w_lo = us[:, o:o + LANES]
    a_lo = us[:, o + LANES:o + 2 * LANES]
    g_lo = us[:, o + 2 * LANES:o + 4 * LANES]
    d = w0_ref[...] + _dot3(jnp.tanh(w_lo), w2h_ref[...], w2l_ref[...])
    w_log = -_softplus(-d) - 0.5
    lw_ref[...] = -jnp.exp(w_log)
    a = _sigmoid(a0_ref[...] + _dot3(a_lo, a2h_ref[...], a2l_ref[...]))
    gate_ref[...] = _dot3(_sigmoid(g_lo), g2h_ref[...], g2l_ref[...])
    kkr = k * kkw_ref[...]
    nrm = jnp.sqrt(_dot_exact_rhs(kkr * kkr, bd_ref[...]))
    kk = kkr / jnp.maximum(nrm, KK_EPS)
    k_eff = k * (1.0 + (a - 1.0) * ka_ref[...])
    r_ref[...] = r
    k_ref[...] = k_eff
    v_ref[...] = v
    kk_ref[...] = kk
    b_ref[...] = kk * a
    bonus_ref[...] = _dot_exact_rhs(r * k_eff * rk_ref[...], bd_ref[...]) * v


def rwkv_prep(u, prev, pw, rw, seq_len, *, tm, halo_mode):
    n, uw = u.shape
    tm = min(tm, n)
    c = lambda i: (0, 0)
    row_spec = pl.BlockSpec((tm, uw), lambda i: (i, 0))
    if halo_mode:
        tps = seq_len // tm
        halo_blk = tm // 8
        lead_specs = [row_spec,
                      pl.BlockSpec((8, uw), lambda i: (jnp.maximum(i * halo_blk - 1, 0), 0)),
                      pl.BlockSpec((1, 1, uw), lambda i: (i // tps, 0, 0))]
        lead = (u, u, prev)
    else:
        tps = 1
        lead_specs = [row_spec, row_spec]
        lead = (u, prev)
    vec = lambda width: pl.BlockSpec((1, width), c)
    mat = lambda rows: pl.BlockSpec((rows, rw), c)
    in_specs = lead_specs + [
        vec(uw), vec(rw), mat(LANES), mat(LANES), vec(rw), mat(LANES), mat(LANES),
        mat(2 * LANES), mat(2 * LANES), vec(rw), vec(rw), vec(rw), pl.BlockSpec(pw['bd'].shape, c)]
    out_spec = pl.BlockSpec((tm, rw), lambda i: (i, 0))
    kern = functools.partial(_rwkv_prep_kernel, rw=rw, halo_mode=halo_mode, tiles_per_seq=tps)
    return pl.pallas_call(
        kern,
        grid=(n // tm,),
        in_specs=in_specs,
        out_specs=[out_spec] * 8,
        out_shape=[jax.ShapeDtypeStruct((n, rw), F32)] * 8,
        compiler_params=_params("parallel"),
        name="rwkv_prep",
    )(*lead, pw['mu'], pw['w0'], pw['w2h'], pw['w2l'], pw['a0'], pw['a2h'], pw['a2l'],
      pw['g2h'], pw['g2l'], pw['kk'], pw['ka'], pw['rk'], pw['bd'])


def _rwkv_chunk_kernel(r_ref, k_ref, v_ref, lw_ref, kk_ref, b_ref, y_ref, s_out_ref, s_ref, *, chunk, npairs):
    c = pl.program_id(1)
    C = chunk
    C2 = HEADS_PER_TILE * C

    @pl.when(c == 0)
    def _():
        s_ref[...] = jnp.zeros_like(s_ref)

    ci = lax.broadcasted_iota(jnp.int32, (C, C), 0)
    cj = lax.broadcasted_iota(jnp.int32, (C, C), 1)
    ti = lax.broadcasted_iota(jnp.int32, (C2, C2), 0)
    tj = lax.broadcasted_iota(jnp.int32, (C2, C2), 1)
    same_blk = (ti // C) == (tj // C)
    incl = same_blk & (ti >= tj)
    strict = same_blk & (ti > tj)
    eye = (ti == tj).astype(F32)
    lane_head = lax.broadcasted_iota(jnp.int32, (C2, LANES), 1) // HEAD_DIM
    row_head = lax.broadcasted_iota(jnp.int32, (C2, LANES), 0) // C
    own_lanes = lane_head == row_head

    def stack(x):
        return jnp.where(own_lanes, jnp.concatenate([x] * HEADS_PER_TILE, axis=0), jnp.zeros((), BF16))

    lw = lw_ref[...]
    l_hi, l_mid, l_lo = _split3(lw)
    tri = (ci >= cj).astype(BF16)
    cl = _dot(tri, l_hi) + (_dot(tri, l_mid) + _dot(tri, l_lo))
    cl_end = cl[C - 1:C, :]
    e_pos = jnp.exp(cl)
    e_neg = jnp.exp(-cl)
    e_prev = jnp.exp(cl - lw)
    e_end = jnp.exp(cl_end - cl)
    all_ones = jnp.ones((C, LANES), BF16)
    d_rows = jnp.exp(_dot_tn(l_hi, all_ones) + (_dot_tn(l_mid, all_ones) + _dot_tn(l_lo, all_ones)))
    r = r_ref[...]
    k = k_ref[...]
    b = b_ref[...]
    rt = (r * e_pos).astype(BF16)
    kt = (kk_ref[...] * e_prev).astype(BF16)
    k_inv = (k * e_neg).astype(BF16)
    b_inv = (b * e_neg).astype(BF16)
    k_dec = (k * e_end).astype(BF16)
    b_dec = (b * e_end).astype(BF16)
    vb = v_ref[...].astype(BF16)

    tiles = range(npairs)
    sls = [slice(p * LANES, (p + 1) * LANES) for p in tiles]
    st = [s_ref[p] for p in tiles]
    stb = [s.astype(BF16) for s in st]
    v2 = [stack(vb[:, sl]) for sl in sls]
    kt2 = [stack(kt[:, sl]) for sl in sls]
    rt2 = [stack(rt[:, sl]) for sl in sls]
    ab = [_dot_nt(jnp.concatenate([kt2[p], rt2[p]], axis=0),
                  jnp.concatenate([stack(k_inv[:, sls[p]]), stack(b_inv[:, sls[p]])], axis=0))
          for p in tiles]
    a_m = [jnp.where(strict, m[:C2, :C2], 0.0).astype(BF16) for m in ab]
    n_m = [jnp.where(strict, -m[:C2, C2:], 0.0) for m in ab]
    aq_m = [jnp.where(incl, m[C2:, :C2], 0.0).astype(BF16) for m in ab]
    nbq_m = [jnp.where(incl, -m[C2:, C2:], 0.0).astype(BF16) for m in ab]
    g = [eye + m for m in n_m]
    pw = [_dot(m.astype(BF16), m.astype(BF16)) for m in n_m]
    levels = int(math.log2(C))
    for lvl in range(1, levels):
        if lvl < levels - 1:
            gp = [_dot(jnp.concatenate([g[p], pw[p]], axis=0).astype(BF16), pw[p].astype(BF16)) for p in tiles]
            g = [g[p] + gp[p][:C2] for p in tiles]
            pw = [m[C2:] for m in gp]
        else:
            g = [g[p] + _dot(g[p].astype(BF16), pw[p].astype(BF16)) for p in tiles]
    x = [_dot(jnp.concatenate([a_m[p], kt2[p]], axis=1), jnp.concatenate([v2[p], stb[p]], axis=0)) for p in tiles]
    u2b = [_dot(g[p].astype(BF16), x[p].astype(BF16)).astype(BF16) for p in tiles]
    y2 = [_dot(jnp.concatenate([aq_m[p], nbq_m[p], rt2[p]], axis=1),
               jnp.concatenate([v2[p], u2b[p], stb[p]], axis=0)) for p in tiles]
    upd = [_dot_tn(jnp.concatenate([stack(k_dec[:, sls[p]]), stack(b_dec[:, sls[p]])], axis=0),
                   jnp.concatenate([v2[p], -u2b[p]], axis=0)) for p in tiles]
    for p in tiles:
        y = y2[p][:C]
        for hh in range(1, HEADS_PER_TILE):
            y = y + y2[p][hh * C:(hh + 1) * C]
        y_ref[:, sls[p]] = y
        s_new = st[p] * d_rows[sls[p], :] + upd[p]
        s_ref[p] = s_new
        s_out_ref[0, p] = s_new


def rwkv_chunk(r, k, v, lw, kk, b, batch, seq_len):
    n, rw = r.shape
    npairs = rw // LANES
    C = RW_CHUNK
    nc = seq_len // C
    blk = pl.BlockSpec((C, rw), lambda i, c: (i * nc + c, 0))
    kern = functools.partial(_rwkv_chunk_kernel, chunk=C, npairs=npairs)
    return pl.pallas_call(
        kern,
        grid=(batch, nc),
        in_specs=[blk] * 6,
        out_specs=[blk, pl.BlockSpec((1, npairs, LANES, LANES), lambda i, c: (i, 0, 0, 0))],
        out_shape=[jax.ShapeDtypeStruct((n, rw), F32),
                   jax.ShapeDtypeStruct((batch, npairs, LANES, LANES), F32)],
        scratch_shapes=[pltpu.VMEM((npairs, LANES, LANES), F32)],
        compiler_params=_params("parallel", "arbitrary"),
        name="rwkv_chunk",
    )(r, k, v, lw, kk, b)


def _rwkv_seq_kernel(s0_ref, r_ref, k_ref, lw_ref, kk_ref, b_ref, v_ref, y_ref, s_out_ref, *, steps):
    s = s0_ref[0]
    dh = s.shape[-1]
    eye = (lax.broadcasted_iota(jnp.int32, (dh, dh), 0) == lax.broadcasted_iota(jnp.int32, (dh, dh), 1)).astype(F32)
    for t in range(steps):
        vcol = jnp.sum(v_ref[0, t] * eye, axis=-1, keepdims=True)
        sa = jnp.sum(s * kk_ref[0, t], axis=-1, keepdims=True)
        s = s * jnp.exp(lw_ref[0, t]) - sa * b_ref[0, t] + vcol * k_ref[0, t]
        ycol = jnp.sum(s * r_ref[0, t], axis=-1, keepdims=True)
        y_ref[0, t] = jnp.sum(ycol * eye, axis=-2, keepdims=True)
    s_out_ref[0] = s


def rwkv_seq(s0, r, k, v, lw, kk, b):
    bsz, heads, dh, _ = s0.shape
    t = r.shape[1]
    rowv = lambda x: x.reshape(bsz, t, heads, 1, dh)
    row_spec = pl.BlockSpec((1, t, heads, 1, dh), lambda i: (i, 0, 0, 0, 0))
    st_spec = pl.BlockSpec((1, heads, dh, dh), lambda i: (i, 0, 0, 0))
    kern = functools.partial(_rwkv_seq_kernel, steps=t)
    y, s = pl.pallas_call(
        kern,
        grid=(bsz,),
        in_specs=[st_spec] + [row_spec] * 6,
        out_specs=[row_spec, st_spec],
        out_shape=[jax.ShapeDtypeStruct((bsz, t, heads, 1, dh), F32),
                   jax.ShapeDtypeStruct(s0.shape, F32)],
        compiler_params=_params("parallel"),
        name="rwkv_seq",
    )(s0, rowv(r), rowv(k), rowv(lw), rowv(kk), rowv(b), rowv(v))
    return y.reshape(bsz, t, heads * dh), s


def _rwkv_epi_kernel(y_ref, bonus_ref, gate_ref, g_ref, b_ref, bd_ref, o_ref):
    y = y_ref[...]
    inv = 1.0 / HEAD_DIM
    mu = _dot_exact_rhs(y, bd_ref[...]) * inv
    yc = y - mu
    var = _dot_exact_rhs(yc * yc, bd_ref[...]) * inv
    yn = yc * lax.rsqrt(var + LNX_EPS) * g_ref[...] + b_ref[...]
    o_ref[...] = ((yn + bonus_ref[...]) * gate_ref[...]).astype(BF16)


def rwkv_epilogue(y, bonus, gate, ln_g, ln_b, bd, *, tm):
    n, rw = y.shape
    tm = min(tm, n)
    blk = pl.BlockSpec((tm, rw), lambda i: (i, 0))
    vec = pl.BlockSpec((1, rw), lambda i: (0, 0))
    return pl.pallas_call(
        _rwkv_epi_kernel,
        grid=(n // tm,),
        in_specs=[blk, blk, blk, vec, vec, pl.BlockSpec(bd.shape, lambda i: (0, 0))],
        out_specs=blk,
        out_shape=jax.ShapeDtypeStruct((n, rw), BF16),
        compiler_params=_params("parallel"),
        name="rwkv_epilogue",
    )(y, bonus, gate, ln_g, ln_b, bd)


def _pad_rows(m, rows):
    return jnp.pad(m, ((0, rows - m.shape[0]), (0, 0)))


def _hi_lo(m):
    hi = m.astype(BF16)
    return hi, (m - hi.astype(F32)).astype(BF16)


def _u_layout(rw, n_decay, n_icl, n_gate):
    segs = [(3 * rw, 3 * rw), (n_decay, LANES), (n_icl, LANES), (n_gate, 2 * LANES)]
    assert n_decay <= LANES and n_icl <= LANES and n_gate <= 2 * LANES
    return segs


def _pad_u_cols(m, segs):
    out, o = [], 0
    for width, padded in segs:
        out.append(m[..., o:o + width])
        if padded > width:
            out.append(jnp.zeros(m.shape[:-1] + (padded - width,), m.dtype))
        o += width
    return jnp.concatenate(out, axis=-1)


def _unpad_u_cols(m, segs):
    out, o = [], 0
    for width, padded in segs:
        out.append(m[..., o:o + width])
        o += padded
    return jnp.concatenate(out, axis=-1)


def _prepare_weights(w_in, w_out, g_pre_mix, g_post_mix, g_pre_ffn, g_post_ffn, g_sb_out, sb_bias,
                     rw_mu, rw_w0, rw_w2, rw_a0, rw_a2, rw_g2, rw_kk, rw_ka, rw_rk, rw_ln_g, rw_ln_b,
                     w_up, conv_w, conv_b, w_down):
    rw = rw_w0.shape[0]
    sbw = g_sb_out.shape[0]
    segs = _u_layout(rw, rw_w2.shape[0], rw_a2.shape[0], rw_g2.shape[0])
    row = lambda x: x.reshape(1, -1)
    w2h, w2l = _hi_lo(_pad_rows(rw_w2, LANES))
    a2h, a2l = _hi_lo(_pad_rows(rw_a2, LANES))
    g2h, g2l = _hi_lo(_pad_rows(rw_g2, 2 * LANES))
    bd = _head_block_ones(HEAD_SUM_TILE)
    w_in_p = jnp.concatenate([w_in[:, :3 * sbw], _pad_u_cols(w_in[:, 3 * sbw:], segs)], axis=1).astype(BF16)
    prep = dict(mu=row(_pad_u_cols(rw_mu, segs)), w0=row(rw_w0), w2h=w2h, w2l=w2l, a0=row(rw_a0),
                a2h=a2h, a2l=a2l, g2h=g2h, g2l=g2l, kk=row(rw_kk), ka=row(rw_ka),
                rk=row(rw_rk), bd=bd)
    return dict(
        segs=segs, rw=rw, sbw=sbw, w_in=w_in_p, w_kv_t=w_in[:, sbw:3 * sbw].T.astype(BF16),
        w_out=w_out.astype(BF16),
        g_pre_mix=row(g_pre_mix), g_post_mix=row(g_post_mix), g_pre_ffn=row(g_pre_ffn),
        g_post_ffn=row(g_post_ffn), g_sb=row(g_sb_out), sb_bias=sb_bias, prep=prep,
        ln_g=row(rw_ln_g), ln_b=row(rw_ln_b), bd=bd,
        w_up=w_up.astype(BF16), conv_w=conv_w, conv_b=row(conv_b), w_down=w_down.astype(BF16))


def _tail(wts, x, o_sb, o_rw, *, tm):
    mixed_in = jnp.concatenate([o_sb, o_rw], axis=-1)
    return mm_norm_resid(mixed_in, wts['w_out'], x, wts['g_post_mix'], wts['g_pre_ffn'],
                         tm=tm, tk=mixed_in.shape[1], emit_h=True)


def _down(wts, act, x1, *, tm):
    (y,) = mm_norm_resid(act, wts['w_down'], x1, wts['g_post_ffn'], wts['g_post_ffn'],
                         tm=tm, tk=act.shape[1] // 4, emit_h=False)
    return y


def _prompt_layer(wts, x3):
    bsz, t, d = x3.shape
    n = bsz * t
    x = x3.reshape(n, d)
    rw, sbw, segs = wts['rw'], wts['sbw'], wts['segs']
    uw = sum(p for _, p in segs)
    q, k_t, v_t, u = in_proj(x, wts['g_pre_mix'], wts['w_in'], sbw, uw, tm=min(1024, t),
                             w_kv_t=wts['w_kv_t'], seq_len=t)
    o_sb = sb_prompt(q, k_t, v_t, wts['sb_bias'], wts['g_sb'], bsz, t, tq=min(512, t), tk=min(256, t))
    shift0 = jnp.zeros((bsz, 1, uw), F32)
    r, ke, vr, lw, kk, b, gate, bonus = rwkv_prep(u, shift0, wts['prep'], rw, t, tm=min(256, t), halo_mode=True)
    y, s_pairs = rwkv_chunk(r, ke, vr, lw, kk, b, bsz, t)
    o_rw = rwkv_epilogue(y, bonus, gate, wts['ln_g'], wts['ln_b'], wts['bd'], tm=512)
    x1, h2 = _tail(wts, x, o_sb, o_rw, tm=512)
    tm_f = min(1024, t)
    conv0 = jnp.zeros((bsz, CONV_W - 1, wts['conv_w'].shape[1]), F32)
    act, gtail = ffn_up_prompt(h2, wts['w_up'], wts['conv_w'], wts['conv_b'], conv0, t, tm=tm_f)
    yout = _down(wts, act, x1, tm=512)
    heads = rw // HEAD_DIM
    hp = HEADS_PER_TILE
    sp = s_pairs.reshape(bsz, rw // LANES, hp, HEAD_DIM, hp, HEAD_DIM)
    wkv = jnp.stack([jnp.swapaxes(sp[:, :, i, :, i, :], -1, -2) for i in range(hp)], axis=2)
    wkv = wkv.reshape(bsz, heads, HEAD_DIM, HEAD_DIM)
    shift = _unpad_u_cols(u.reshape(bsz, t, uw)[:, -1], segs)
    tps = t // tm_f
    conv_new = gtail.reshape(bsz, tps, 8, -1)[:, -1, 8 - (CONV_W - 1):]
    keys_first = lambda a: jnp.moveaxis(a.reshape(bsz, sbw // HEAD_DIM, HEAD_DIM, t), -1, 1)
    return yout.reshape(bsz, t, d), keys_first(k_t), keys_first(v_t), wkv, shift, conv_new


def _sample_layer(wts, x3, cache_k, cache_v, layer, page_table, wkv0, shift_prev, conv_prev):
    bsz, t, d = x3.shape
    n = bsz * t
    x = x3.reshape(n, d)
    rw, sbw, segs = wts['rw'], wts['sbw'], wts['segs']
    uw = sum(p for _, p in segs)
    q, k, v, u = in_proj(x, wts['g_pre_mix'], wts['w_in'], sbw, uw, tm=n)
    kvshape = (bsz, t, sbw // HEAD_DIM, HEAD_DIM)
    o_sb = sb_sample(q.reshape(kvshape), k.reshape(kvshape), v.reshape(kvshape),
                     wts['sb_bias'], wts['g_sb'], cache_k, cache_v, layer, page_table).reshape(n, sbw)
    u3 = u.reshape(bsz, t, uw)
    u_prev = jnp.concatenate([_pad_u_cols(shift_prev, segs)[:, None], u3[:, :-1]], axis=1).reshape(n, uw)
    r, ke, vr, lw, kk, b, gate, bonus = rwkv_prep(u, u_prev, wts['prep'], rw, t, tm=n, halo_mode=False)
    as3 = lambda a: a.reshape(bsz, t, rw)
    y3, wkv = rwkv_seq(wkv0, as3(r), as3(ke), as3(vr), as3(lw), as3(kk), as3(b))
    o_rw = rwkv_epilogue(y3.reshape(n, rw), bonus, gate, wts['ln_g'], wts['ln_b'], wts['bd'], tm=n)
    x1, h2 = _tail(wts, x, o_sb, o_rw, tm=n)
    f = wts['conv_w'].shape[1]
    zeros = jnp.zeros((bsz, t, f), F32)
    p1 = zeros.at[:, 0].set(conv_prev[:, 1]).reshape(n, f)
    p2 = zeros.at[:, 0].set(conv_prev[:, 0]).at[:, 1].set(conv_prev[:, 1]).reshape(n, f)
    act, gfull = ffn_up_sample(h2, wts['w_up'], wts['conv_w'], wts['conv_b'], p1, p2, t)
    yout = _down(wts, act, x1, tm=n)
    shift = _unpad_u_cols(u3[:, -1], segs)
    conv_new = gfull.reshape(bsz, t, f)[:, t - (CONV_W - 1):]
    kvshape = (bsz, t, sbw // HEAD_DIM, HEAD_DIM)
    return yout.reshape(bsz, t, d), k.reshape(kvshape), v.reshape(kvshape), wkv, shift, conv_new


def kernel(x_prompt, x_sample, cache_sb_k, cache_sb_v, state_rwkv_wkv, state_rwkv_shift, state_ffn_conv, page_table, w_in, w_out, g_pre_mix, g_post_mix, g_pre_ffn, g_post_ffn, g_sb_out, sb_bias, rw_mu, rw_w0, rw_w2, rw_a0, rw_a2, rw_g2, rw_kk, rw_ka, rw_rk, rw_ln_g, rw_ln_b, w_up, conv_w, conv_b, w_down):
    depth = w_in.shape[0]
    yp, ys = x_prompt, x_sample
    outs_p, outs_s = [], []
    for l in range(depth):
        wts = _prepare_weights(w_in[l], w_out[l], g_pre_mix[l], g_post_mix[l], g_pre_ffn[l], g_post_ffn[l],
                               g_sb_out[l], sb_bias[l], rw_mu[l], rw_w0[l], rw_w2[l], rw_a0[l], rw_a2[l],
                               rw_g2[l], rw_kk[l], rw_ka[l], rw_rk[l].reshape(-1), rw_ln_g[l], rw_ln_b[l],
                               w_up[l], conv_w[l], conv_b[l], w_down[l])
        yp, *rest_p = _prompt_layer(wts, yp)
        ys, *rest_s = _sample_layer(wts, ys, cache_sb_k, cache_sb_v, l, page_table,
                                    state_rwkv_wkv[l], state_rwkv_shift[l], state_ffn_conv[l])
        outs_p.append(rest_p)
        outs_s.append(rest_s)
    stack = lambda outs, i: jnp.stack([o[i] for o in outs])
    return (yp, ys,
            stack(outs_p, 0), stack(outs_p, 1), stack(outs_p, 2), stack(outs_p, 3), stack(outs_p, 4),
            stack(outs_s, 0), stack(outs_s, 1), stack(outs_s, 2), stack(outs_s, 3), stack(outs_s, 4))
```

```python
import functools
import math

import jax
import jax.numpy as jnp
from jax import lax
from jax.experimental import pallas as pl
from jax.experimental.pallas import tpu as pltpu

F32 = jnp.float32
BF16 = jnp.bfloat16

HEAD_DIM = 64
LANES = 128
HEADS_PER_TILE = LANES // HEAD_DIM
NORM_EPS = 1e-6
LNX_EPS = 64e-5
KK_EPS = 1e-12
CONV_W = 3
PAGE_SIZE = 128
VMEM_LIMIT = 48 * 1024 * 1024
RW_CHUNK = 64
HEAD_SUM_TILE = 256


def _dot(a, b):
    return jnp.dot(a, b, preferred_element_type=F32)


def _dot_nt(a, b):
    return lax.dot_general(a, b, (((1,), (1,)), ((), ())), preferred_element_type=F32)


def _dot_tn(a, b):
    return lax.dot_general(a, b, (((0,), (0,)), ((), ())), preferred_element_type=F32)


def _split2(x):
    hi = x.astype(BF16)
    lo = (x - hi.astype(F32)).astype(BF16)
    return hi, lo


def _split3(x):
    hi = x.astype(BF16)
    r1 = x - hi.astype(F32)
    mid = r1.astype(BF16)
    lo = (r1 - mid.astype(F32)).astype(BF16)
    return hi, mid, lo


def _dot_exact_rhs(x, m01):
    hi, lo = _split2(x)
    t = m01.shape[0]
    cols = [slice(c * t, (c + 1) * t) for c in range(x.shape[1] // t)]
    his = [_dot(hi[:, c], m01) for c in cols]
    los = [_dot(lo[:, c], m01) for c in cols]
    out = [h + l for h, l in zip(his, los)]
    return out[0] if len(out) == 1 else jnp.concatenate(out, axis=1)


def _dot3(a, b_hi, b_lo):
    a_hi, a_lo = _split2(a)
    return _dot(a_hi, b_hi) + (_dot(a_lo, b_hi) + _dot(a_hi, b_lo))


def _softplus(x):
    return jnp.maximum(x, 0.0) + jnp.log(1.0 + jnp.exp2(jnp.abs(x) * (-math.log2(math.e))))


def _sigmoid(x):
    return 1.0 / (1.0 + jnp.exp(-x))


def _rms_rows(x, g):
    return x * lax.rsqrt(jnp.mean(x * x, axis=-1, keepdims=True) + NORM_EPS) * g


def _params(*sem):
    return pltpu.CompilerParams(dimension_semantics=sem, vmem_limit_bytes=VMEM_LIMIT)


def _head_block_ones(n):
    i = jnp.arange(n) // HEAD_DIM
    return (i[:, None] == i[None, :]).astype(BF16)


def _in_proj_kernel(x_ref, g_ref, w_ref, *rest, nq, nk, nv, kv_t):
    if kv_t:
        wt_ref, q_ref, k_ref, v_ref, u_ref, h_ref = rest
    else:
        q_ref, k_ref, v_ref, u_ref, h_ref = rest
    j = pl.program_id(1)

    @pl.when(j == 0)
    def _():
        h_ref[...] = _rms_rows(x_ref[...], g_ref[...]).astype(BF16)

    def kv_block():
        return _dot_nt(wt_ref[...], h_ref[...]) if kv_t else _dot(h_ref[...], w_ref[...])

    @pl.when(j < nq)
    def _():
        q_ref[...] = (_dot(h_ref[...], w_ref[...]) * (HEAD_DIM ** -0.5)).astype(BF16)

    @pl.when((j >= nq) & (j < nq + nk))
    def _():
        k_ref[...] = kv_block()

    @pl.when((j >= nq + nk) & (j < nq + nk + nv))
    def _():
        v_ref[...] = kv_block()

    @pl.when(j >= nq + nk + nv)
    def _():
        u_ref[...] = _dot(h_ref[...], w_ref[...])


def in_proj(x, g, w, sb_width, u_width, *, tm, tn=512, w_kv_t=None, seq_len=None):
    n, d = x.shape
    tm = min(tm, n)
    nq = nk = nv = sb_width // tn
    nu = u_width // tn
    nj = nq + nk + nv + nu
    off_k, off_v, off_u = nq, nq + nk, nq + nk + nv
    clip = lambda j, lo, cnt: jnp.clip(j - lo, 0, cnt - 1)
    kv_t = w_kv_t is not None
    kern = functools.partial(_in_proj_kernel, nq=nq, nk=nk, nv=nv, kv_t=kv_t)
    in_specs = [
        pl.BlockSpec((tm, d), lambda i, j: (i, 0)),
        pl.BlockSpec((1, d), lambda i, j: (0, 0)),
        pl.BlockSpec((d, tn), lambda i, j: (0, j)),
    ]
    operands = (x, g, w)
    if kv_t:
        tps = seq_len // tm
        in_specs.append(pl.BlockSpec((tn, d), lambda i, j: (clip(j, off_k, nk + nv), 0)))
        operands += (w_kv_t,)
        kv_specs = [pl.BlockSpec((None, tn, tm), lambda i, j: (i // tps, clip(j, off_k, nk), i % tps)),
                    pl.BlockSpec((None, tn, tm), lambda i, j: (i // tps, clip(j, off_v, nv), i % tps))]
        kv_shape = jax.ShapeDtypeStruct((n // seq_len, sb_width, seq_len), F32)
    else:
        kv_specs = [pl.BlockSpec((tm, tn), lambda i, j: (i, clip(j, off_k, nk))),
                    pl.BlockSpec((tm, tn), lambda i, j: (i, clip(j, off_v, nv)))]
        kv_shape = jax.ShapeDtypeStruct((n, sb_width), F32)
    return pl.pallas_call(
        kern,
        grid=(n // tm, nj),
        in_specs=in_specs,
        out_specs=[pl.BlockSpec((tm, tn), lambda i, j: (i, clip(j, 0, nq)))] + kv_specs
                  + [pl.BlockSpec((tm, tn), lambda i, j: (i, clip(j, off_u, nu)))],
        out_shape=[jax.ShapeDtypeStruct((n, sb_width), BF16), kv_shape, kv_shape,
                   jax.ShapeDtypeStruct((n, u_width), F32)],
        scratch_shapes=[pltpu.VMEM((tm, d), BF16)],
        compiler_params=_params("parallel", "arbitrary"),
        name="in_proj",
    )(*operands)


def _mm_norm_resid_kernel(a_ref, w_ref, r_ref, g_ref, g2_ref, *rest, nk, emit_h):
    if emit_h:
        o_ref, h_ref, acc_ref = rest
    else:
        o_ref, acc_ref = rest
        h_ref = None
    kk = pl.program_id(1)

    def finalize(m):
        y = r_ref[...] + _rms_rows(m, g_ref[...])
        o_ref[...] = y
        if emit_h:
            h_ref[...] = _rms_rows(y, g2_ref[...]).astype(BF16)

    def k_step(j):
        part = _dot(a_ref[...], w_ref[...])
        if nk == 1:
            finalize(part)
        elif j == 0:
            acc_ref[...] = part
        elif j < nk - 1:
            acc_ref[...] += part
        else:
            finalize(acc_ref[...] + part)

    if nk == 1:
        k_step(0)
    else:
        for j in range(nk):
            pl.when(kk == j)(functools.partial(k_step, j))


def mm_norm_resid(a, w, resid, g, g2, *, tm, tk, emit_h):
    n, kdim = a.shape
    d = w.shape[1]
    tm = min(tm, n)
    nk = kdim // tk
    kern = functools.partial(_mm_norm_resid_kernel, nk=nk, emit_h=emit_h)
    out_specs = [pl.BlockSpec((tm, d), lambda i, k: (i, 0))]
    out_shape = [jax.ShapeDtypeStruct((n, d), F32)]
    if emit_h:
        out_specs.append(pl.BlockSpec((tm, d), lambda i, k: (i, 0)))
        out_shape.append(jax.ShapeDtypeStruct((n, d), BF16))
    return pl.pallas_call(
        kern,
        grid=(n // tm, nk),
        in_specs=[
            pl.BlockSpec((tm, tk), lambda i, k: (i, k)),
            pl.BlockSpec((tk, d), lambda i, k: (k, 0)),
            pl.BlockSpec((tm, d), lambda i, k: (i, 0)),
            pl.BlockSpec((1, d), lambda i, k: (0, 0)),
            pl.BlockSpec((1, d), lambda i, k: (0, 0)),
        ],
        out_specs=out_specs,
        out_shape=out_shape,
        scratch_shapes=[pltpu.VMEM((tm, d), F32)],
        compiler_params=_params("parallel", "arbitrary"),
        name="mm_norm_resid",
    )(a, w, resid, g, g2)


def _gelu_tanh(x):
    c = math.sqrt(2.0 / math.pi)
    return x * (0.5 * (1.0 + jnp.tanh(c * (x + 0.044715 * (x * x * x)))))


def _ffn_up_kernel(*refs, tiles_per_seq, seq_len, halo_mode):
    if halo_mode:
        h_ref, halo_ref, wg_ref, wv_ref, cw_ref, cb_ref, prev_ref, act_ref, g_out_ref = refs
    else:
        h_ref, wg_ref, wv_ref, cw_ref, cb_ref, p1_ref, p2_ref, act_ref, g_out_ref = refs
    h = h_ref[...]
    g = _dot(h, wg_ref[...])
    val = _dot(h, wv_ref[...])
    tm = g.shape[0]
    row = lax.broadcasted_iota(jnp.int32, (tm, 1), 0)
    r1 = pltpu.roll(g, 1, axis=0)
    r2 = pltpu.roll(g, 2, axis=0)
    if halo_mode:
        gh = _dot(halo_ref[...], wg_ref[...])
        first = (pl.program_id(0) % tiles_per_seq) == 0
        pm1 = jnp.where(first, prev_ref[0, 1:2, :], gh[7:8, :])
        pm2 = jnp.where(first, prev_ref[0, 0:1, :], gh[6:7, :])
        g1 = jnp.where(row == 0, pm1, r1)
        g2 = jnp.where(row == 0, pm2, jnp.where(row == 1, pm1, r2))
        g_out_ref[0] = g[tm - 8:, :]
    else:
        t = row % seq_len
        g1 = jnp.where(t >= 1, r1, p1_ref[...])
        g2 = jnp.where(t >= 2, r2, p2_ref[...])
        g_out_ref[...] = g
    conv = cb_ref[...] + cw_ref[0:1, :] * g2 + cw_ref[1:2, :] * g1 + cw_ref[2:3, :] * g
    act_ref[...] = (_gelu_tanh(conv) * val).astype(BF16)


def ffn_up_prompt(h, w_up, conv_w, conv_b, conv_prev, seq_len, *, tm, tn=512):
    n, d = h.shape
    f = w_up.shape[1] // 2
    nf = f // tn
    tps = seq_len // tm
    kern = functools.partial(_ffn_up_kernel, tiles_per_seq=tps, seq_len=seq_len, halo_mode=True)
    halo_blk = tm // 8
    return pl.pallas_call(
        kern,
        grid=(n // tm, nf),
        in_specs=[
            pl.BlockSpec((tm, d), lambda i, j: (i, 0)),
            pl.BlockSpec((8, d), lambda i, j: (jnp.maximum(i * halo_blk - 1, 0), 0)),
            pl.BlockSpec((d, tn), lambda i, j: (0, j)),
            pl.BlockSpec((d, tn), lambda i, j: (0, j + nf)),
            pl.BlockSpec((CONV_W, tn), lambda i, j: (0, j)),
            pl.BlockSpec((1, tn), lambda i, j: (0, j)),
            pl.BlockSpec((1, CONV_W - 1, tn), lambda i, j: (i // tps, 0, j)),
        ],
        out_specs=[
            pl.BlockSpec((tm, tn), lambda i, j: (i, j)),
            pl.BlockSpec((1, 8, tn), lambda i, j: (i, 0, j)),
        ],
        out_shape=[
            jax.ShapeDtypeStruct((n, f), BF16),
            jax.ShapeDtypeStruct((n // tm, 8, f), F32),
        ],
        compiler_params=_params("parallel", "arbitrary"),
        name="ffn_up_prompt",
    )(h, h, w_up, w_up, conv_w, conv_b, conv_prev)


def ffn_up_sample(h, w_up, conv_w, conv_b, p1, p2, seq_len, *, tn=512):
    n, d = h.shape
    f = w_up.shape[1] // 2
    nf = f // tn
    kern = functools.partial(_ffn_up_kernel, tiles_per_seq=1, seq_len=seq_len, halo_mode=False)
    return pl.pallas_call(
        kern,
        grid=(1, nf),
        in_specs=[
            pl.BlockSpec((n, d), lambda i, j: (0, 0)),
            pl.BlockSpec((d, tn), lambda i, j: (0, j)),
            pl.BlockSpec((d, tn), lambda i, j: (0, j + nf)),
            pl.BlockSpec((CONV_W, tn), lambda i, j: (0, j)),
            pl.BlockSpec((1, tn), lambda i, j: (0, j)),
            pl.BlockSpec((n, tn), lambda i, j: (0, j)),
            pl.BlockSpec((n, tn), lambda i, j: (0, j)),
        ],
        out_specs=[
            pl.BlockSpec((n, tn), lambda i, j: (0, j)),
            pl.BlockSpec((n, tn), lambda i, j: (0, j)),
        ],
        out_shape=[
            jax.ShapeDtypeStruct((n, f), BF16),
            jax.ShapeDtypeStruct((n, f), F32),
        ],
        compiler_params=_params("arbitrary", "arbitrary"),
        name="ffn_up_sample",
    )(h, w_up, w_up, conv_w, conv_b, p1, p2)


def _sb_prompt_kernel(bias_ref, q_ref, k_ref, v_ref, later_ref, bd_ref, g_ref, o_ref,
                      kb_ref, vb_ref, carry_ref, acc_ref, *, tq, tk):
    p = pl.program_id(1)
    qi = pl.program_id(2)
    ratio = tq // tk

    @pl.when(qi == 0)
    def _():
        kb_ref[...] = k_ref[...].astype(BF16)
        vb_ref[...] = v_ref[...].astype(BF16)

    q = q_ref[...]
    lane = lax.broadcasted_iota(jnp.int32, (1, LANES), 1)
    qms = [jnp.where((lane // HEAD_DIM) == hh, q, jnp.zeros_like(q)) for hh in range(HEADS_PER_TILE)]
    biases = [bias_ref[p * HEADS_PER_TILE + hh] for hh in range(HEADS_PER_TILE)]
    carry_ref[...] = jnp.zeros_like(carry_ref)
    acc_ref[...] = jnp.zeros_like(acc_ref)
    later = later_ref[...]

    hs = range(HEADS_PER_TILE)

    def key_tiles(r0, nr, tile_ids, causal_first):
        j0 = [pl.multiple_of(j * tk, tk) for j in tile_ids]
        n = len(j0)
        rows = slice(r0, r0 + nr)
        qs = [qms[hh][rows] for hh in hs]
        kblk = [kb_ref[:, pl.ds(s, tk)] for s in j0]
        vblk = [vb_ref[:, pl.ds(s, tk)] for s in j0]
        mask = [None] * n
        if causal_first:
            mask[0] = lax.broadcasted_iota(jnp.int32, (nr, tk), 1) < lax.broadcasted_iota(jnp.int32, (nr, tk), 0)
        masked = lambda i, x: x if mask[i] is None else jnp.where(mask[i], x, 0.0)
        z = [[_dot(qs[hh], kblk[i]) + biases[hh] for hh in hs] for i in range(n)]
        sp = [[masked(i, _softplus(z[i][hh])) for hh in hs] for i in range(n)]
        ls = [[z[i][hh] - sp[i][hh] for hh in hs] for i in range(n)]
        after = [[_dot(s.astype(BF16), later) for s in sp[i]] for i in range(n)]
        total = [[jnp.sum(s, axis=-1, keepdims=True) for s in sp[i]] for i in range(n)]
        c = [carry_ref[hh, rows, :] for hh in hs]
        pv = [None] * HEADS_PER_TILE
        for i in range(n):
            for hh in hs:
                w = masked(i, jnp.exp(ls[i][hh] - after[i][hh] - jnp.concatenate([c[hh]] * (tk // LANES), axis=1)))
                term = _dot_nt(w.astype(BF16), vblk[i])
                pv[hh] = term if pv[hh] is None else pv[hh] + term
                c[hh] = c[hh] + total[i][hh]
        for hh in hs:
            acc_ref[hh, rows, :] += pv[hh]
            carry_ref[hh, rows, :] = c[hh]

    for rb in range(ratio - 1, -1, -1):
        key_tiles(rb * tk, tk, [qi * ratio + d for d in range(rb, -1, -1)], True)

    def body(it, c):
        jq = qi - 1 - it
        key_tiles(0, tq, [jq * ratio + d for d in range(ratio - 1, -1, -1)], False)
        return c

    lax.fori_loop(0, qi, body, 0)
    o = acc_ref[0]
    for hh in range(1, HEADS_PER_TILE):
        o = jnp.where((lane // HEAD_DIM) == hh, acc_ref[hh], o)
    ms = _dot_exact_rhs(o * o, bd_ref[...]) * (1.0 / HEAD_DIM)
    o_ref[...] = (o * lax.rsqrt(ms + NORM_EPS) * g_ref[...]).astype(BF16)


def _later_matrix(n):
    idx = jnp.arange(n)
    return (idx[:, None] > idx[None, :]).astype(BF16)


def sb_prompt(q, k, v, bias, g_sb, batch, seq_len, *, tq, tk):
    n, w = q.shape
    npairs = w // LANES
    nq = seq_len // tq
    kern = functools.partial(_sb_prompt_kernel, tq=tq, tk=tk)
    const = lambda b, p, i: (0, 0)
    return pl.pallas_call(
        kern,
        grid=(batch, npairs, nq),
        in_specs=[
            pl.BlockSpec(memory_space=pltpu.SMEM),
            pl.BlockSpec((tq, LANES), lambda b, p, i: (b * nq + i, p)),
            pl.BlockSpec((None, LANES, seq_len), lambda b, p, i: (b, p, 0)),
            pl.BlockSpec((None, LANES, seq_len), lambda b, p, i: (b, p, 0)),
            pl.BlockSpec((tk, tk), const),
            pl.BlockSpec((LANES, LANES), const),
            pl.BlockSpec((1, LANES), lambda b, p, i: (0, p)),
        ],
        out_specs=pl.BlockSpec((tq, LANES), lambda b, p, i: (b * nq + i, p)),
        out_shape=jax.ShapeDtypeStruct((n, w), BF16),
        scratch_shapes=[
            pltpu.VMEM((LANES, seq_len), BF16),
            pltpu.VMEM((LANES, seq_len), BF16),
            pltpu.VMEM((HEADS_PER_TILE, tq, LANES), F32),
            pltpu.VMEM((HEADS_PER_TILE, tq, LANES), F32),
        ],
        compiler_params=_params("parallel", "parallel", "arbitrary"),
        name="sb_prompt",
    )(bias, q, k, v, _later_matrix(tk), _head_block_ones(LANES), g_sb)


SAMPLE_T_PAD = 8
PAGES_PER_STEP = 8


def _sb_sample_kernel(pt_ref, q_ref, bias_ref, g_ref, later_ref, ones_ref, hm_ref, bd_ref, *rest, heads, pps):
    del pt_ref
    kn, vn = rest[0:1], rest[1:2]
    kp, vp = rest[2:2 + pps], rest[2 + pps:2 + 2 * pps]
    o_ref, carry_ref, acc_ref = rest[2 + 2 * pps:]
    pg = pl.program_id(1)
    qx = q_ref[0]
    rows = qx.shape[0]
    bias = bias_ref[...]
    later = later_ref[...]
    ones = ones_ref[...]

    def sweep(k_refs, v_refs, carry, acc, mask):
        n = len(k_refs)
        z = [_dot(qx, k_refs[i][...].astype(BF16)) + bias for i in range(n)]
        sp = [_softplus(x) for x in z]
        if mask is not None:
            sp = [jnp.where(mask, s, 0.0) for s in sp]
        spb = [s.astype(BF16) for s in sp]
        after = [_dot(s, later) for s in spb]
        total = [_dot(s, ones) for s in spb]
        w = []
        for i in range(n):
            wi = jnp.exp(z[i] - sp[i] - after[i] - carry)
            w.append(wi if mask is None else jnp.where(mask, wi, 0.0))
            carry = carry + total[i]
        pv = [_dot_nt(w[i].astype(BF16), v_refs[i][...].astype(BF16)) for i in range(n)]
        for i in range(n):
            acc = acc + pv[i]
        return carry, acc

    def store_state(carry, acc):
        carry_ref[...] = carry
        acc_ref[...] = acc

    @pl.when(pg == 0)
    def _():
        t = lax.broadcasted_iota(jnp.int32, (rows, PAGE_SIZE), 0) // heads
        s = lax.broadcasted_iota(jnp.int32, (rows, PAGE_SIZE), 1)
        store_state(*sweep(kn, vn, jnp.zeros((rows, PAGE_SIZE), F32), jnp.zeros(acc_ref.shape, F32), s < t))

    store_state(*sweep(kp, vp, carry_ref[...], acc_ref[...], None))

    @pl.when(pg == pl.num_programs(1) - 1)
    def _():
        a = acc_ref[...] * hm_ref[...]
        o = jnp.sum(a.reshape(rows // heads, heads, a.shape[1]), axis=1)
        ms = _dot_exact_rhs(o * o, bd_ref[...]) * (1.0 / HEAD_DIM)
        o_ref[0] = (o * lax.rsqrt(ms + NORM_EPS) * g_ref[...]).astype(BF16)


def sb_sample(q, k_new, v_new, bias, g_sb, cache_k, cache_v, layer, page_table):
    b, t, heads, dh = q.shape
    w = heads * dh
    tp = SAMPLE_T_PAD
    pps = PAGES_PER_STEP
    n_pages = page_table.shape[1]
    rows = tp * heads
    keys_last = lambda x: jnp.moveaxis(x, -3, -1).reshape(x.shape[:-3] + (w, x.shape[-3]))
    new_page = lambda x: jnp.pad(keys_last(x), ((0, 0), (0, 0), (0, PAGE_SIZE - t)))
    head_of_col = jnp.arange(w) // dh
    head_of_row = jnp.arange(rows) % heads
    hmask = head_of_row[:, None] == head_of_col[None, :]
    qp = jnp.pad(q.reshape(b, t, w), ((0, 0), (0, tp - t), (0, 0)))
    qx = jnp.where(hmask[None], jnp.repeat(qp, heads, axis=1), jnp.zeros((), BF16))
    bias_full = jnp.broadcast_to(bias[head_of_row][:, None], (rows, PAGE_SIZE)).astype(F32)
    kern = functools.partial(_sb_sample_kernel, heads=heads, pps=pps)
    c2 = lambda i, p, pt: (0, 0)
    per_b = lambda i, p, pt: (i, 0, 0)
    new_spec = pl.BlockSpec((None, w, PAGE_SIZE), per_b)
    page = lambda s: (lambda i, p, pt: (layer, pt[i, n_pages - 1 - (p * pps + s)], 0, 0))
    page_specs = [pl.BlockSpec((None, None, w, PAGE_SIZE), page(s)) for s in range(pps)]
    grid_spec = pltpu.PrefetchScalarGridSpec(
        num_scalar_prefetch=1,
        grid=(b, n_pages // pps),
        in_specs=[
            pl.BlockSpec((1, rows, w), per_b),
            pl.BlockSpec((rows, PAGE_SIZE), c2),
            pl.BlockSpec((1, w), c2),
            pl.BlockSpec((PAGE_SIZE, PAGE_SIZE), c2),
            pl.BlockSpec((PAGE_SIZE, PAGE_SIZE), c2),
            pl.BlockSpec((rows, w), c2),
            pl.BlockSpec((HEAD_SUM_TILE, HEAD_SUM_TILE), c2),
            new_spec, new_spec,
        ] + page_specs + page_specs,
        out_specs=pl.BlockSpec((1, tp, w), per_b),
        scratch_shapes=[
            pltpu.VMEM((rows, PAGE_SIZE), F32),
            pltpu.VMEM((rows, w), F32),
        ],
    )
    o = pl.pallas_call(
        kern,
        grid_spec=grid_spec,
        out_shape=jax.ShapeDtypeStruct((b, tp, w), BF16),
        compiler_params=_params("parallel", "arbitrary"),
        name="sb_sample",
    )(page_table, qx, bias_full, g_sb, _later_matrix(PAGE_SIZE), jnp.ones((PAGE_SIZE, PAGE_SIZE), BF16),
      hmask.astype(F32), _head_block_ones(HEAD_SUM_TILE), new_page(k_new), new_page(v_new),
      *([keys_last(cache_k)] * pps), *([keys_last(cache_v)] * pps))
    return o[:, :t].reshape(b, t, heads, dh)


def _rwkv_prep_kernel(*refs, rw, halo_mode, tiles_per_seq):
    if halo_mode:
        u_ref, halo_ref, st_ref = refs[:3]
        rest = refs[3:]
    else:
        u_ref, up_ref = refs[:2]
        rest = refs[2:]
    (mu_ref, w0_ref, w2h_ref, w2l_ref, a0_ref, a2h_ref, a2l_ref, g2h_ref, g2l_ref,
     kkw_ref, ka_ref, rk_ref, bd_ref,
     r_ref, k_ref, v_ref, lw_ref, kk_ref, b_ref, gate_ref, bonus_ref) = rest
    u = u_ref[...]
    if halo_mode:
        tm = u.shape[0]
        row = lax.broadcasted_iota(jnp.int32, (tm, 1), 0)
        first = (pl.program_id(0) % tiles_per_seq) == 0
        prev_row = jnp.where(first, st_ref[0], halo_ref[7:8, :])
        up = jnp.where(row == 0, prev_row, pltpu.roll(u, 1, axis=0))
    else:
        up = up_ref[...]
    us = u + (up - u) * mu_ref[...]
    r = us[:, 0:rw]
    k = us[:, rw:2 * rw]
    v = us[:, 2 * rw:3 * rw]
    o = 3 * rw
    w_lo = us[:, o:o + LANES]
    a_lo = us[:, o + LANES:o + 2 * LANES]
    g_lo = us[:, o + 2 * LANES:o + 4 * LANES]
    d = w0_ref[...] + _dot3(jnp.tanh(w_lo), w2h_ref[...], w2l_ref[...])
    w_log = -_softplus(-d) - 0.5
    lw_ref[...] = -jnp.exp(w_log)
    a = _sigmoid(a0_ref[...] + _dot3(a_lo, a2h_ref[...], a2l_ref[...]))
    gate_ref[...] = _dot3(_sigmoid(g_lo), g2h_ref[...], g2l_ref[...])
    kkr = k * kkw_ref[...]
    nrm = jnp.sqrt(_dot_exact_rhs(kkr * kkr, bd_ref[...]))
    kk = kkr / jnp.maximum(nrm, KK_EPS)
    k_eff = k * (1.0 + (a - 1.0) * ka_ref[...])
    r_ref[...] = r
    k_ref[...] = k_eff
    v_ref[...] = v
    kk_ref[...] = kk
    b_ref[...] = kk * a
    bonus_ref[...] = _dot_exact_rhs(r * k_eff * rk_ref[...], bd_ref[...]) * v


def rwkv_prep(u, prev, pw, rw, seq_len, *, tm, halo_mode):
    n, uw = u.shape
    tm = min(tm, n)
    c = lambda i: (0, 0)
    row_spec = pl.BlockSpec((tm, uw), lambda i: (i, 0))
    if halo_mode:
        tps = seq_len // tm
        halo_blk = tm // 8
        lead_specs = [row_spec,
                      pl.BlockSpec((8, uw), lambda i: (jnp.maximum(i * halo_blk - 1, 0), 0)),
                      pl.BlockSpec((1, 1, uw), lambda i: (i // tps, 0, 0))]
        lead = (u, u, prev)
    else:
        tps = 1
        lead_specs = [row_spec, row_spec]
        lead = (u, prev)
    vec = lambda width: pl.BlockSpec((1, width), c)
    mat = lambda rows: pl.BlockSpec((rows, rw), c)
    in_specs = lead_specs + [
        vec(uw), vec(rw), mat(LANES), mat(LANES), vec(rw), mat(LANES), mat(LANES),
        mat(2 * LANES), mat(2 * LANES), vec(rw), vec(rw), vec(rw), pl.BlockSpec(pw['bd'].shape, c)]
    out_spec = pl.BlockSpec((tm, rw), lambda i: (i, 0))
    kern = functools.partial(_rwkv_prep_kernel, rw=rw, halo_mode=halo_mode, tiles_per_seq=tps)
    return pl.pallas_call(
        kern,
        grid=(n // tm,),
        in_specs=in_specs,
        out_specs=[out_spec] * 8,
        out_shape=[jax.ShapeDtypeStruct((n, rw), F32)] * 8,
        compiler_params=_params("parallel"),
        name="rwkv_prep",
    )(*lead, pw['mu'], pw['w0'], pw['w2h'], pw['w2l'], pw['a0'], pw['a2h'], pw['a2l'],
      pw['g2h'], pw['g2l'], pw['kk'], pw['ka'], pw['rk'], pw['bd'])


def _rwkv_chunk_kernel(r_ref, k_ref, v_ref, lw_ref, kk_ref, b_ref, y_ref, s_out_ref, s_ref, *, chunk, npairs):
    c = pl.program_id(1)
    C = chunk
    C2 = HEADS_PER_TILE * C

    @pl.when(c == 0)
    def _():
        s_ref[...] = jnp.zeros_like(s_ref)

    ci = lax.broadcasted_iota(jnp.int32, (C, C), 0)
    cj = lax.broadcasted_iota(jnp.int32, (C, C), 1)
    ti = lax.broadcasted_iota(jnp.int32, (C2, C2), 0)
    tj = lax.broadcasted_iota(jnp.int32, (C2, C2), 1)
    same_blk = (ti // C) == (tj // C)
    incl = same_blk & (ti >= tj)
    strict = same_blk & (ti > tj)
    eye = (ti == tj).astype(F32)
    lane_head = lax.broadcasted_iota(jnp.int32, (C2, LANES), 1) // HEAD_DIM
    row_head = lax.broadcasted_iota(jnp.int32, (C2, LANES), 0) // C
    own_lanes = lane_head == row_head

    def stack(x):
        return jnp.where(own_lanes, jnp.concatenate([x] * HEADS_PER_TILE, axis=0), jnp.zeros((), BF16))

    lw = lw_ref[...]
    l_hi, l_mid, l_lo = _split3(lw)
    tri = (ci >= cj).astype(BF16)
    cl = _dot(tri, l_hi) + (_dot(tri, l_mid) + _dot(tri, l_lo))
    cl_end = cl[C - 1:C, :]
    e_pos = jnp.exp(cl)
    e_neg = jnp.exp(-cl)
    e_prev = jnp.exp(cl - lw)
    e_end = jnp.exp(cl_end - cl)
    all_ones = jnp.ones((C, LANES), BF16)
    d_rows = jnp.exp(_dot_tn(l_hi, all_ones) + (_dot_tn(l_mid, all_ones) + _dot_tn(l_lo, all_ones)))
    r = r_ref[...]
    k = k_ref[...]
    b = b_ref[...]
    rt = (r * e_pos).astype(BF16)
    kt = (kk_ref[...] * e_prev).astype(BF16)
    k_inv = (k * e_neg).astype(BF16)
    b_inv = (b * e_neg).astype(BF16)
    k_dec = (k * e_end).astype(BF16)
    b_dec = (b * e_end).astype(BF16)
    vb = v_ref[...].astype(BF16)

    tiles = range(npairs)
    sls = [slice(p * LANES, (p + 1) * LANES) for p in tiles]
    st = [s_ref[p] for p in tiles]
    stb = [s.astype(BF16) for s in st]
    v2 = [stack(vb[:, sl]) for sl in sls]
    kt2 = [stack(kt[:, sl]) for sl in sls]
    rt2 = [stack(rt[:, sl]) for sl in sls]
    ab = [_dot_nt(jnp.concatenate([kt2[p], rt2[p]], axis=0),
                  jnp.concatenate([stack(k_inv[:, sls[p]]), stack(b_inv[:, sls[p]])], axis=0))
          for p in tiles]
    a_m = [jnp.where(strict, m[:C2, :C2], 0.0).astype(BF16) for m in ab]
    n_m = [jnp.where(strict, -m[:C2, C2:], 0.0) for m in ab]
    aq_m = [jnp.where(incl, m[C2:, :C2], 0.0).astype(BF16) for m in ab]
    nbq_m = [jnp.where(incl, -m[C2:, C2:], 0.0).astype(BF16) for m in ab]
    g = [eye + m for m in n_m]
    pw = [_dot(m.astype(BF16), m.astype(BF16)) for m in n_m]
    levels = int(math.log2(C))
    for lvl in range(1, levels):
        if lvl < levels - 1:
            gp = [_dot(jnp.concatenate([g[p], pw[p]], axis=0).astype(BF16), pw[p].astype(BF16)) for p in tiles]
            g = [g[p] + gp[p][:C2] for p in tiles]
            pw = [m[C2:] for m in gp]
        else:
            g = [g[p] + _dot(g[p].astype(BF16), pw[p].astype(BF16)) for p in tiles]
    x = [_dot(jnp.concatenate([a_m[p], kt2[p]], axis=1), jnp.concatenate([v2[p], stb[p]], axis=0)) for p in tiles]
    u2b = [_dot(g[p].astype(BF16), x[p].astype(BF16)).astype(BF16) for p in tiles]
    y2 = [_dot(jnp.concatenate([aq_m[p], nbq_m[p], rt2[p]], axis=1),
               jnp.concatenate([v2[p], u2b[p], stb[p]], axis=0)) for p in tiles]
    upd = [_dot_tn(jnp.concatenate([stack(k_dec[:, sls[p]]), stack(b_dec[:, sls[p]])], axis=0),
                   jnp.concatenate([v2[p], -u2b[p]], axis=0)) for p in tiles]
    for p in tiles:
        y = y2[p][:C]
        for hh in range(1, HEADS_PER_TILE):
            y = y + y2[p][hh * C:(hh + 1) * C]
        y_ref[:, sls[p]] = y
        s_new = st[p] * d_rows[sls[p], :] + upd[p]
        s_ref[p] = s_new
        s_out_ref[0, p] = s_new


def rwkv_chunk(r, k, v, lw, kk, b, batch, seq_len):
    n, rw = r.shape
    npairs = rw // LANES
    C = RW_CHUNK
    nc = seq_len // C
    blk = pl.BlockSpec((C, rw), lambda i, c: (i * nc + c, 0))
    kern = functools.partial(_rwkv_chunk_kernel, chunk=C, npairs=npairs)
    return pl.pallas_call(
        kern,
        grid=(batch, nc),
        in_specs=[blk] * 6,
        out_specs=[blk, pl.BlockSpec((1, npairs, LANES, LANES), lambda i, c: (i, 0, 0, 0))],
        out_shape=[jax.ShapeDtypeStruct((n, rw), F32),
                   jax.ShapeDtypeStruct((batch, npairs, LANES, LANES), F32)],
        scratch_shapes=[pltpu.VMEM((npairs, LANES, LANES), F32)],
        compiler_params=_params("parallel", "arbitrary"),
        name="rwkv_chunk",
    )(r, k, v, lw, kk, b)


def _rwkv_seq_kernel(s0_ref, r_ref, k_ref, lw_ref, kk_ref, b_ref, v_ref, y_ref, s_out_ref, *, steps):
    s = s0_ref[0]
    dh = s.shape[-1]
    eye = (lax.broadcasted_iota(jnp.int32, (dh, dh), 0) == lax.broadcasted_iota(jnp.int32, (dh, dh), 1)).astype(F32)
    for t in range(steps):
        vcol = jnp.sum(v_ref[0, t] * eye, axis=-1, keepdims=True)
        sa = jnp.sum(s * kk_ref[0, t], axis=-1, keepdims=True)
        s = s * jnp.exp(lw_ref[0, t]) - sa * b_ref[0, t] + vcol * k_ref[0, t]
        ycol = jnp.sum(s * r_ref[0, t], axis=-1, keepdims=True)
        y_ref[0, t] = jnp.sum(ycol * eye, axis=-2, keepdims=True)
    s_out_ref[0] = s


def rwkv_seq(s0, r, k, v, lw, kk, b):
    bsz, heads, dh, _ = s0.shape
    t = r.shape[1]
    rowv = lambda x: x.reshape(bsz, t, heads, 1, dh)
    row_spec = pl.BlockSpec((1, t, heads, 1, dh), lambda i: (i, 0, 0, 0, 0))
    st_spec = pl.BlockSpec((1, heads, dh, dh), lambda i: (i, 0, 0, 0))
    kern = functools.partial(_rwkv_seq_kernel, steps=t)
    y, s = pl.pallas_call(
        kern,
        grid=(bsz,),
        in_specs=[st_spec] + [row_spec] * 6,
        out_specs=[row_spec, st_spec],
        out_shape=[jax.ShapeDtypeStruct((bsz, t, heads, 1, dh), F32),
                   jax.ShapeDtypeStruct(s0.shape, F32)],
        compiler_params=_params("parallel"),
        name="rwkv_seq",
    )(s0, rowv(r), rowv(k), rowv(lw), rowv(kk), rowv(b), rowv(v))
    return y.reshape(bsz, t, heads * dh), s


def _rwkv_epi_kernel(y_ref, bonus_ref, gate_ref, g_ref, b_ref, bd_ref, o_ref):
    y = y_ref[...]
    inv = 1.0 / HEAD_DIM
    mu = _dot_exact_rhs(y, bd_ref[...]) * inv
    yc = y - mu
    var = _dot_exact_rhs(yc * yc, bd_ref[...]) * inv
    yn = yc * lax.rsqrt(var + LNX_EPS) * g_ref[...] + b_ref[...]
    o_ref[...] = ((yn + bonus_ref[...]) * gate_ref[...]).astype(BF16)


def rwkv_epilogue(y, bonus, gate, ln_g, ln_b, bd, *, tm):
    n, rw = y.shape
    tm = min(tm, n)
    blk = pl.BlockSpec((tm, rw), lambda i: (i, 0))
    vec = pl.BlockSpec((1, rw), lambda i: (0, 0))
    return pl.pallas_call(
        _rwkv_epi_kernel,
        grid=(n // tm,),
        in_specs=[blk, blk, blk, vec, vec, pl.BlockSpec(bd.shape, lambda i: (0, 0))],
        out_specs=blk,
        out_shape=jax.ShapeDtypeStruct((n, rw), BF16),
        compiler_params=_params("parallel"),
        name="rwkv_epilogue",
    )(y, bonus, gate, ln_g, ln_b, bd)


def _pad_rows(m, rows):
    return jnp.pad(m, ((0, rows - m.shape[0]), (0, 0)))


def _hi_lo(m):
    hi = m.astype(BF16)
    return hi, (m - hi.astype(F32)).astype(BF16)


def _u_layout(rw, n_decay, n_icl, n_gate):
    segs = [(3 * rw, 3 * rw), (n_decay, LANES), (n_icl, LANES), (n_gate, 2 * LANES)]
    assert n_decay <= LANES and n_icl <= LANES and n_gate <= 2 * LANES
    return segs


def _pad_u_cols(m, segs):
    out, o = [], 0
    for width, padded in segs:
        out.append(m[..., o:o + width])
        if padded > width:
            out.append(jnp.zeros(m.shape[:-1] + (padded - width,), m.dtype))
        o += width
    return jnp.concatenate(out, axis=-1)


def _unpad_u_cols(m, segs):
    out, o = [], 0
    for width, padded in segs:
        out.append(m[..., o:o + width])
        o += padded
    return jnp.concatenate(out, axis=-1)


def _prepare_weights(w_in, w_out, g_pre_mix, g_post_mix, g_pre_ffn, g_post_ffn, g_sb_out, sb_bias,
                     rw_mu, rw_w0, rw_w2, rw_a0, rw_a2, rw_g2, rw_kk, rw_ka, rw_rk, rw_ln_g, rw_ln_b,
                     w_up, conv_w, conv_b, w_down):
    rw = rw_w0.shape[0]
    sbw = g_sb_out.shape[0]
    segs = _u_layout(rw, rw_w2.shape[0], rw_a2.shape[0], rw_g2.shape[0])
    row = lambda x: x.reshape(1, -1)
    w2h, w2l = _hi_lo(_pad_rows(rw_w2, LANES))
    a2h, a2l = _hi_lo(_pad_rows(rw_a2, LANES))
    g2h, g2l = _hi_lo(_pad_rows(rw_g2, 2 * LANES))
    bd = _head_block_ones(HEAD_SUM_TILE)
    w_in_p = jnp.concatenate([w_in[:, :3 * sbw], _pad_u_cols(w_in[:, 3 * sbw:], segs)], axis=1).astype(BF16)
    prep = dict(mu=row(_pad_u_cols(rw_mu, segs)), w0=row(rw_w0), w2h=w2h, w2l=w2l, a0=row(rw_a0),
                a2h=a2h, a2l=a2l, g2h=g2h, g2l=g2l, kk=row(rw_kk), ka=row(rw_ka),
                rk=row(rw_rk), bd=bd)
    return dict(
        segs=segs, rw=rw, sbw=sbw, w_in=w_in_p, w_kv_t=w_in[:, sbw:3 * sbw].T.astype(BF16),
        w_out=w_out.astype(BF16),
        g_pre_mix=row(g_pre_mix), g_post_mix=row(g_post_mix), g_pre_ffn=row(g_pre_ffn),
        g_post_ffn=row(g_post_ffn), g_sb=row(g_sb_out), sb_bias=sb_bias, prep=prep,
        ln_g=row(rw_ln_g), ln_b=row(rw_ln_b), bd=bd,
        w_up=w_up.astype(BF16), conv_w=conv_w, conv_b=row(conv_b), w_down=w_down.astype(BF16))


def _tail(wts, x, o_sb, o_rw, *, tm):
    mixed_in = jnp.concatenate([o_sb, o_rw], axis=-1)
    return mm_norm_resid(mixed_in, wts['w_out'], x, wts['g_post_mix'], wts['g_pre_ffn'],
                         tm=tm, tk=mixed_in.shape[1], emit_h=True)


def _down(wts, act, x1, *, tm):
    (y,) = mm_norm_resid(act, wts['w_down'], x1, wts['g_post_ffn'], wts['g_post_ffn'],
                         tm=tm, tk=act.shape[1] // 4, emit_h=False)
    return y


def _prompt_layer(wts, x3):
    bsz, t, d = x3.shape
    n = bsz * t
    x = x3.reshape(n, d)
    rw, sbw, segs = wts['rw'], wts['sbw'], wts['segs']
    uw = sum(p for _, p in segs)
    q, k_t, v_t, u = in_proj(x, wts['g_pre_mix'], wts['w_in'], sbw, uw, tm=min(1024, t),
                             w_kv_t=wts['w_kv_t'], seq_len=t)
    o_sb = sb_prompt(q, k_t, v_t, wts['sb_bias'], wts['g_sb'], bsz, t, tq=min(512, t), tk=min(256, t))
    shift0 = jnp.zeros((bsz, 1, uw), F32)
    r, ke, vr, lw, kk, b, gate, bonus = rwkv_prep(u, shift0, wts['prep'], rw, t, tm=min(256, t), halo_mode=True)
    y, s_pairs = rwkv_chunk(r, ke, vr, lw, kk, b, bsz, t)
    o_rw = rwkv_epilogue(y, bonus, gate, wts['ln_g'], wts['ln_b'], wts['bd'], tm=512)
    x1, h2 = _tail(wts, x, o_sb, o_rw, tm=512)
    tm_f = min(1024, t)
    conv0 = jnp.zeros((bsz, CONV_W - 1, wts['conv_w'].shape[1]), F32)
    act, gtail = ffn_up_prompt(h2, wts['w_up'], wts['conv_w'], wts['conv_b'], conv0, t, tm=tm_f)
    yout = _down(wts, act, x1, tm=512)
    heads = rw // HEAD_DIM
    hp = HEADS_PER_TILE
    sp = s_pairs.reshape(bsz, rw // LANES, hp, HEAD_DIM, hp, HEAD_DIM)
    wkv = jnp.stack([jnp.swapaxes(sp[:, :, i, :, i, :], -1, -2) for i in range(hp)], axis=2)
    wkv = wkv.reshape(bsz, heads, HEAD_DIM, HEAD_DIM)
    shift = _unpad_u_cols(u.reshape(bsz, t, uw)[:, -1], segs)
    tps = t // tm_f
    conv_new = gtail.reshape(bsz, tps, 8, -1)[:, -1, 8 - (CONV_W - 1):]
    keys_first = lambda a: jnp.moveaxis(a.reshape(bsz, sbw // HEAD_DIM, HEAD_DIM, t), -1, 1)
    return yout.reshape(bsz, t, d), keys_first(k_t), keys_first(v_t), wkv, shift, conv_new


def _sample_layer(wts, x3, cache_k, cache_v, layer, page_table, wkv0, shift_prev, conv_prev):
    bsz, t, d = x3.shape
    n = bsz * t
    x = x3.reshape(n, d)
    rw, sbw, segs = wts['rw'], wts['sbw'], wts['segs']
    uw = sum(p for _, p in segs)
    q, k, v, u = in_proj(x, wts['g_pre_mix'], wts['w_in'], sbw, uw, tm=n)
    kvshape = (bsz, t, sbw // HEAD_DIM, HEAD_DIM)
    o_sb = sb_sample(q.reshape(kvshape), k.reshape(kvshape), v.reshape(kvshape),
                     wts['sb_bias'], wts['g_sb'], cache_k, cache_v, layer, page_table).reshape(n, sbw)
    u3 = u.reshape(bsz, t, uw)
    u_prev = jnp.concatenate([_pad_u_cols(shift_prev, segs)[:, None], u3[:, :-1]], axis=1).reshape(n, uw)
    r, ke, vr, lw, kk, b, gate, bonus = rwkv_prep(u, u_prev, wts['prep'], rw, t, tm=n, halo_mode=False)
    as3 = lambda a: a.reshape(bsz, t, rw)
    y3, wkv = rwkv_seq(wkv0, as3(r), as3(ke), as3(vr), as3(lw), as3(kk), as3(b))
    o_rw = rwkv_epilogue(y3.reshape(n, rw), bonus, gate, wts['ln_g'], wts['ln_b'], wts['bd'], tm=n)
    x1, h2 = _tail(wts, x, o_sb, o_rw, tm=n)
    f = wts['conv_w'].shape[1]
    zeros = jnp.zeros((bsz, t, f), F32)
    p1 = zeros.at[:, 0].set(conv_prev[:, 1]).reshape(n, f)
    p2 = zeros.at[:, 0].set(conv_prev[:, 0]).at[:, 1].set(conv_prev[:, 1]).reshape(n, f)
    act, gfull = ffn_up_sample(h2, wts['w_up'], wts['conv_w'], wts['conv_b'], p1, p2, t)
    yout = _down(wts, act, x1, tm=n)
    shift = _unpad_u_cols(u3[:, -1], segs)
    conv_new = gfull.reshape(bsz, t, f)[:, t - (CONV_W - 1):]
    kvshape = (bsz, t, sbw // HEAD_DIM, HEAD_DIM)
    return yout.reshape(bsz, t, d), k.reshape(kvshape), v.reshape(kvshape), wkv, shift, conv_new


def kernel(x_prompt, x_sample, cache_sb_k, cache_sb_v, state_rwkv_wkv, state_rwkv_shift, state_ffn_conv, page_table, w_in, w_out, g_pre_mix, g_post_mix, g_pre_ffn, g_post_ffn, g_sb_out, sb_bias, rw_mu, rw_w0, rw_w2, rw_a0, rw_a2, rw_g2, rw_kk, rw_ka, rw_rk, rw_ln_g, rw_ln_b, w_up, conv_w, conv_b, w_down):
    depth = w_in.shape[0]
    yp, ys = x_prompt, x_sample
    outs_p, outs_s = [], []
    for l in range(depth):
        wts = _prepare_weights(w_in[l], w_out[l], g_pre_mix[l], g_post_mix[l], g_pre_ffn[l], g_post_ffn[l],
                               g_sb_out[l], sb_bias[l], rw_mu[l], rw_w0[l], rw_w2[l], rw_a0[l], rw_a2[l],
                               rw_g2[l], rw_kk[l], rw_ka[l], rw_rk[l].reshape(-1), rw_ln_g[l], rw_ln_b[l],
                               w_up[l], conv_w[l], conv_b[l], w_down[l])
        yp, *rest_p = _prompt_layer(wts, yp)
        ys, *rest_s = _sample_layer(wts, ys, cache_sb_k, cache_sb_v, l, page_table,
                                    state_rwkv_wkv[l], state_rwkv_shift[l], state_ffn_conv[l])
        outs_p.append(rest_p)
        outs_s.append(rest_s)
    stack = lambda outs, i: jnp.stack([o[i] for o in outs])
    return (yp, ys,
            stack(outs_p, 0), stack(outs_p, 1), stack(outs_p, 2), stack(outs_p, 3), stack(outs_p, 4),
            stack(outs_s, 0), stack(outs_s, 1), stack(outs_s, 2), stack(outs_s, 3), stack(outs_s, 4))
```

```python
import functools
import math

import jax
import jax.numpy as jnp
from jax import lax
from jax.experimental import pallas as pl
from jax.experimental.pallas import tpu as pltpu

F32 = jnp.float32
BF16 = jnp.bfloat16

HEAD_DIM = 64
LANES = 128
HEADS_PER_TILE = LANES // HEAD_DIM
NORM_EPS = 1e-6
LNX_EPS = 64e-5
KK_EPS = 1e-12
CONV_W = 3
PAGE_SIZE = 128
VMEM_LIMIT = 48 * 1024 * 1024
RW_CHUNK = 64
HEAD_SUM_TILE = 256


def _dot(a, b):
    return jnp.dot(a, b, preferred_element_type=F32)


def _dot_nt(a, b):
    return lax.dot_general(a, b, (((1,), (1,)), ((), ())), preferred_element_type=F32)


def _dot_tn(a, b):
    return lax.dot_general(a, b, (((0,), (0,)), ((), ())), preferred_element_type=F32)


def _split2(x):
    hi = x.astype(BF16)
    lo = (x - hi.astype(F32)).astype(BF16)
    return hi, lo


def _split3(x):
    hi = x.astype(BF16)
    r1 = x - hi.astype(F32)
    mid = r1.astype(BF16)
    lo = (r1 - mid.astype(F32)).astype(BF16)
    return hi, mid, lo


def _dot_exact_rhs(x, m01):
    hi, lo = _split2(x)
    t = m01.shape[0]
    cols = [slice(c * t, (c + 1) * t) for c in range(x.shape[1] // t)]
    his = [_dot(hi[:, c], m01) for c in cols]
    los = [_dot(lo[:, c], m01) for c in cols]
    out = [h + l for h, l in zip(his, los)]
    return out[0] if len(out) == 1 else jnp.concatenate(out, axis=1)


def _dot3(a, b_hi, b_lo):
    a_hi, a_lo = _split2(a)
    return _dot(a_hi, b_hi) + (_dot(a_lo, b_hi) + _dot(a_hi, b_lo))


def _softplus(x):
    return jnp.maximum(x, 0.0) + jnp.log(1.0 + jnp.exp2(jnp.abs(x) * (-math.log2(math.e))))


def _sigmoid(x):
    return 1.0 / (1.0 + jnp.exp(-x))


def _rms_rows(x, g):
    return x * lax.rsqrt(jnp.mean(x * x, axis=-1, keepdims=True) + NORM_EPS) * g


def _params(*sem):
    return pltpu.CompilerParams(dimension_semantics=sem, vmem_limit_bytes=VMEM_LIMIT)


def _head_block_ones(n):
    i = jnp.arange(n) // HEAD_DIM
    return (i[:, None] == i[None, :]).astype(BF16)


def _in_proj_kernel(x_ref, g_ref, w_ref, *rest, nq, nk, nv, kv_t):
    if kv_t:
        wt_ref, q_ref, k_ref, v_ref, u_ref, h_ref = rest
    else:
        q_ref, k_ref, v_ref, u_ref, h_ref = rest
    j = pl.program_id(1)

    @pl.when(j == 0)
    def _():
        h_ref[...] = _rms_rows(x_ref[...], g_ref[...]).astype(BF16)

    def kv_block():
        return _dot_nt(wt_ref[...], h_ref[...]) if kv_t else _dot(h_ref[...], w_ref[...])

    @pl.when(j < nq)
    def _():
        q_ref[...] = (_dot(h_ref[...], w_ref[...]) * (HEAD_DIM ** -0.5)).astype(BF16)

    @pl.when((j >= nq) & (j < nq + nk))
    def _():
        k_ref[...] = kv_block()

    @pl.when((j >= nq + nk) & (j < nq + nk + nv))
    def _():
        v_ref[...] = kv_block()

    @pl.when(j >= nq + nk + nv)
    def _():
        u_ref[...] = _dot(h_ref[...], w_ref[...])


def in_proj(x, g, w, sb_width, u_width, *, tm, tn=512, w_kv_t=None, seq_len=None):
    n, d = x.shape
    tm = min(tm, n)
    nq = nk = nv = sb_width // tn
    nu = u_width // tn
    nj = nq + nk + nv + nu
    off_k, off_v, off_u = nq, nq + nk, nq + nk + nv
    clip = lambda j, lo, cnt: jnp.clip(j - lo, 0, cnt - 1)
    kv_t = w_kv_t is not None
    kern = functools.partial(_in_proj_kernel, nq=nq, nk=nk, nv=nv, kv_t=kv_t)
    in_specs = [
        pl.BlockSpec((tm, d), lambda i, j: (i, 0)),
        pl.BlockSpec((1, d), lambda i, j: (0, 0)),
        pl.BlockSpec((d, tn), lambda i, j: (0, j)),
    ]
    operands = (x, g, w)
    if kv_t:
        tps = seq_len // tm
        in_specs.append(pl.BlockSpec((tn, d), lambda i, j: (clip(j, off_k, nk + nv), 0)))
        operands += (w_kv_t,)
        kv_specs = [pl.BlockSpec((None, tn, tm), lambda i, j: (i // tps, clip(j, off_k, nk), i % tps)),
                    pl.BlockSpec((None, tn, tm), lambda i, j: (i // tps, clip(j, off_v, nv), i % tps))]
        kv_shape = jax.ShapeDtypeStruct((n // seq_len, sb_width, seq_len), F32)
    else:
        kv_specs = [pl.BlockSpec((tm, tn), lambda i, j: (i, clip(j, off_k, nk))),
                    pl.BlockSpec((tm, tn), lambda i, j: (i, clip(j, off_v, nv)))]
        kv_shape = jax.ShapeDtypeStruct((n, sb_width), F32)
    return pl.pallas_call(
        kern,
        grid=(n // tm, nj),
        in_specs=in_specs,
        out_specs=[pl.BlockSpec((tm, tn), lambda i, j: (i, clip(j, 0, nq)))] + kv_specs
                  + [pl.BlockSpec((tm, tn), lambda i, j: (i, clip(j, off_u, nu)))],
        out_shape=[jax.ShapeDtypeStruct((n, sb_width), BF16), kv_shape, kv_shape,
                   jax.ShapeDtypeStruct((n, u_width), F32)],
        scratch_shapes=[pltpu.VMEM((tm, d), BF16)],
        compiler_params=_params("parallel", "arbitrary"),
        name="in_proj",
    )(*operands)


def _mm_norm_resid_kernel(a_ref, w_ref, r_ref, g_ref, g2_ref, *rest, nk, emit_h):
    if emit_h:
        o_ref, h_ref, acc_ref = rest
    else:
        o_ref, acc_ref = rest
        h_ref = None
    kk = pl.program_id(1)

    def finalize(m):
        y = r_ref[...] + _rms_rows(m, g_ref[...])
        o_ref[...] = y
        if emit_h:
            h_ref[...] = _rms_rows(y, g2_ref[...]).astype(BF16)

    def k_step(j):
        part = _dot(a_ref[...], w_ref[...])
        if nk == 1:
            finalize(part)
        elif j == 0:
            acc_ref[...] = part
        elif j < nk - 1:
            acc_ref[...] += part
        else:
            finalize(acc_ref[...] + part)

    if nk == 1:
        k_step(0)
    else:
        for j in range(nk):
            pl.when(kk == j)(functools.partial(k_step, j))


def mm_norm_resid(a, w, resid, g, g2, *, tm, tk, emit_h):
    n, kdim = a.shape
    d = w.shape[1]
    tm = min(tm, n)
    nk = kdim // tk
    kern = functools.partial(_mm_norm_resid_kernel, nk=nk, emit_h=emit_h)
    out_specs = [pl.BlockSpec((tm, d), lambda i, k: (i, 0))]
    out_shape = [jax.ShapeDtypeStruct((n, d), F32)]
    if emit_h:
        out_specs.append(pl.BlockSpec((tm, d), lambda i, k: (i, 0)))
        out_shape.append(jax.ShapeDtypeStruct((n, d), BF16))
    return pl.pallas_call(
        kern,
        grid=(n // tm, nk),
        in_specs=[
            pl.BlockSpec((tm, tk), lambda i, k: (i, k)),
            pl.BlockSpec((tk, d), lambda i, k: (k, 0)),
            pl.BlockSpec((tm, d), lambda i, k: (i, 0)),
            pl.BlockSpec((1, d), lambda i, k: (0, 0)),
            pl.BlockSpec((1, d), lambda i, k: (0, 0)),
        ],
        out_specs=out_specs,
        out_shape=out_shape,
        scratch_shapes=[pltpu.VMEM((tm, d), F32)],
        compiler_params=_params("parallel", "arbitrary"),
        name="mm_norm_resid",
    )(a, w, resid, g, g2)


def _gelu_tanh(x):
    c = math.sqrt(2.0 / math.pi)
    return x * (0.5 * (1.0 + jnp.tanh(c * (x + 0.044715 * (x * x * x)))))


def _ffn_up_kernel(*refs, tiles_per_seq, seq_len, halo_mode):
    if halo_mode:
        h_ref, halo_ref, wg_ref, wv_ref, cw_ref, cb_ref, prev_ref, act_ref, g_out_ref = refs
    else:
        h_ref, wg_ref, wv_ref, cw_ref, cb_ref, p1_ref, p2_ref, act_ref, g_out_ref = refs
    h = h_ref[...]
    g = _dot(h, wg_ref[...])
    val = _dot(h, wv_ref[...])
    tm = g.shape[0]
    row = lax.broadcasted_iota(jnp.int32, (tm, 1), 0)
    r1 = pltpu.roll(g, 1, axis=0)
    r2 = pltpu.roll(g, 2, axis=0)
    if halo_mode:
        gh = _dot(halo_ref[...], wg_ref[...])
        first = (pl.program_id(0) % tiles_per_seq) == 0
        pm1 = jnp.where(first, prev_ref[0, 1:2, :], gh[7:8, :])
        pm2 = jnp.where(first, prev_ref[0, 0:1, :], gh[6:7, :])
        g1 = jnp.where(row == 0, pm1, r1)
        g2 = jnp.where(row == 0, pm2, jnp.where(row == 1, pm1, r2))
        g_out_ref[0] = g[tm - 8:, :]
    else:
        t = row % seq_len
        g1 = jnp.where(t >= 1, r1, p1_ref[...])
        g2 = jnp.where(t >= 2, r2, p2_ref[...])
        g_out_ref[...] = g
    conv = cb_ref[...] + cw_ref[0:1, :] * g2 + cw_ref[1:2, :] * g1 + cw_ref[2:3, :] * g
    act_ref[...] = (_gelu_tanh(conv) * val).astype(BF16)


def ffn_up_prompt(h, w_up, conv_w, conv_b, conv_prev, seq_len, *, tm, tn=512):
    n, d = h.shape
    f = w_up.shape[1] // 2
    nf = f // tn
    tps = seq_len // tm
    kern = functools.partial(_ffn_up_kernel, tiles_per_seq=tps, seq_len=seq_len, halo_mode=True)
    halo_blk = tm // 8
    return pl.pallas_call(
        kern,
        grid=(n // tm, nf),
        in_specs=[
            pl.BlockSpec((tm, d), lambda i, j: (i, 0)),
            pl.BlockSpec((8, d), lambda i, j: (jnp.maximum(i * halo_blk - 1, 0), 0)),
            pl.BlockSpec((d, tn), lambda i, j: (0, j)),
            pl.BlockSpec((d, tn), lambda i, j: (0, j + nf)),
            pl.BlockSpec((CONV_W, tn), lambda i, j: (0, j)),
            pl.BlockSpec((1, tn), lambda i, j: (0, j)),
            pl.BlockSpec((1, CONV_W - 1, tn), lambda i, j: (i // tps, 0, j)),
        ],
        out_specs=[
            pl.BlockSpec((tm, tn), lambda i, j: (i, j)),
            pl.BlockSpec((1, 8, tn), lambda i, j: (i, 0, j)),
        ],
        out_shape=[
            jax.ShapeDtypeStruct((n, f), BF16),
            jax.ShapeDtypeStruct((n // tm, 8, f), F32),
        ],
        compiler_params=_params("parallel", "arbitrary"),
        name="ffn_up_prompt",
    )(h, h, w_up, w_up, conv_w, conv_b, conv_prev)


def ffn_up_sample(h, w_up, conv_w, conv_b, p1, p2, seq_len, *, tn=512):
    n, d = h.shape
    f = w_up.shape[1] // 2
    nf = f // tn
    kern = functools.partial(_ffn_up_kernel, tiles_per_seq=1, seq_len=seq_len, halo_mode=False)
    return pl.pallas_call(
        kern,
        grid=(1, nf),
        in_specs=[
            pl.BlockSpec((n, d), lambda i, j: (0, 0)),
            pl.BlockSpec((d, tn), lambda i, j: (0, j)),
            pl.BlockSpec((d, tn), lambda i, j: (0, j + nf)),
            pl.BlockSpec((CONV_W, tn), lambda i, j: (0, j)),
            pl.BlockSpec((1, tn), lambda i, j: (0, j)),
            pl.BlockSpec((n, tn), lambda i, j: (0, j)),
            pl.BlockSpec((n, tn), lambda i, j: (0, j)),
        ],
        out_specs=[
            pl.BlockSpec((n, tn), lambda i, j: (0, j)),
            pl.BlockSpec((n, tn), lambda i, j: (0, j)),
        ],
        out_shape=[
            jax.ShapeDtypeStruct((n, f), BF16),
            jax.ShapeDtypeStruct((n, f), F32),
        ],
        compiler_params=_params("arbitrary", "arbitrary"),
        name="ffn_up_sample",
    )(h, w_up, w_up, conv_w, conv_b, p1, p2)


def _sb_prompt_kernel(bias_ref, q_ref, k_ref, v_ref, later_ref, bd_ref, g_ref, o_ref,
                      kb_ref, vb_ref, carry_ref, acc_ref, *, tq, tk):
    p = pl.program_id(1)
    qi = pl.program_id(2)
    ratio = tq // tk

    @pl.when(qi == 0)
    def _():
        kb_ref[...] = k_ref[...].astype(BF16)
        vb_ref[...] = v_ref[...].astype(BF16)

    q = q_ref[...]
    lane = lax.broadcasted_iota(jnp.int32, (1, LANES), 1)
    qms = [jnp.where((lane // HEAD_DIM) == hh, q, jnp.zeros_like(q)) for hh in range(HEADS_PER_TILE)]
    biases = [bias_ref[p * HEADS_PER_TILE + hh] for hh in range(HEADS_PER_TILE)]
    carry_ref[...] = jnp.zeros_like(carry_ref)
    acc_ref[...] = jnp.zeros_like(acc_ref)
    later = later_ref[...]

    hs = range(HEADS_PER_TILE)

    def key_tiles(r0, nr, tile_ids, causal_first):
        j0 = [pl.multiple_of(j * tk, tk) for j in tile_ids]
        n = len(j0)
        rows = slice(r0, r0 + nr)
        qs = [qms[hh][rows] for hh in hs]
        kblk = [kb_ref[:, pl.ds(s, tk)] for s in j0]
        vblk = [vb_ref[:, pl.ds(s, tk)] for s in j0]
        mask = [None] * n
        if causal_first:
            mask[0] = lax.broadcasted_iota(jnp.int32, (nr, tk), 1) < lax.broadcasted_iota(jnp.int32, (nr, tk), 0)
        masked = lambda i, x: x if mask[i] is None else jnp.where(mask[i], x, 0.0)
        z = [[_dot(qs[hh], kblk[i]) + biases[hh] for hh in hs] for i in range(n)]
        sp = [[masked(i, _softplus(z[i][hh])) for hh in hs] for i in range(n)]
        ls = [[z[i][hh] - sp[i][hh] for hh in hs] for i in range(n)]
        after = [[_dot(s.astype(BF16), later) for s in sp[i]] for i in range(n)]
        total = [[jnp.sum(s, axis=-1, keepdims=True) for s in sp[i]] for i in range(n)]
        c = [carry_ref[hh, rows, :] for hh in hs]
        pv = [None] * HEADS_PER_TILE
        for i in range(n):
            for hh in hs:
                w = masked(i, jnp.exp(ls[i][hh] - after[i][hh] - jnp.concatenate([c[hh]] * (tk // LANES), axis=1)))
                term = _dot_nt(w.astype(BF16), vblk[i])
                pv[hh] = term if pv[hh] is None else pv[hh] + term
                c[hh] = c[hh] + total[i][hh]
        for hh in hs:
            acc_ref[hh, rows, :] += pv[hh]
            carry_ref[hh, rows, :] = c[hh]

    for rb in range(ratio - 1, -1, -1):
        key_tiles(rb * tk, tk, [qi * ratio + d for d in range(rb, -1, -1)], True)

    def body(it, c):
        jq = qi - 1 - it
        key_tiles(0, tq, [jq * ratio + d for d in range(ratio - 1, -1, -1)], False)
        return c

    lax.fori_loop(0, qi, body, 0)
    o = acc_ref[0]
    for hh in range(1, HEADS_PER_TILE):
        o = jnp.where((lane // HEAD_DIM) == hh, acc_ref[hh], o)
    ms = _dot_exact_rhs(o * o, bd_ref[...]) * (1.0 / HEAD_DIM)
    o_ref[...] = (o * lax.rsqrt(ms + NORM_EPS) * g_ref[...]).astype(BF16)


def _later_matrix(n):
    idx = jnp.arange(n)
    return (idx[:, None] > idx[None, :]).astype(BF16)


def sb_prompt(q, k, v, bias, g_sb, batch, seq_len, *, tq, tk):
    n, w = q.shape
    npairs = w // LANES
    nq = seq_len // tq
    kern = functools.partial(_sb_prompt_kernel, tq=tq, tk=tk)
    const = lambda b, p, i: (0, 0)
    return pl.pallas_call(
        kern,
        grid=(batch, npairs, nq),
        in_specs=[
            pl.BlockSpec(memory_space=pltpu.SMEM),
            pl.BlockSpec((tq, LANES), lambda b, p, i: (b * nq + i, p)),
            pl.BlockSpec((None, LANES, seq_len), lambda b, p, i: (b, p, 0)),
            pl.BlockSpec((None, LANES, seq_len), lambda b, p, i: (b, p, 0)),
            pl.BlockSpec((tk, tk), const),
            pl.BlockSpec((LANES, LANES), const),
            pl.BlockSpec((1, LANES), lambda b, p, i: (0, p)),
        ],
        out_specs=pl.BlockSpec((tq, LANES), lambda b, p, i: (b * nq + i, p)),
        out_shape=jax.ShapeDtypeStruct((n, w), BF16),
        scratch_shapes=[
            pltpu.VMEM((LANES, seq_len), BF16),
            pltpu.VMEM((LANES, seq_len), BF16),
            pltpu.VMEM((HEADS_PER_TILE, tq, LANES), F32),
            pltpu.VMEM((HEADS_PER_TILE, tq, LANES), F32),
        ],
        compiler_params=_params("parallel", "parallel", "arbitrary"),
        name="sb_prompt",
    )(bias, q, k, v, _later_matrix(tk), _head_block_ones(LANES), g_sb)


SAMPLE_T_PAD = 8
PAGES_PER_STEP = 8


def _sb_sample_kernel(pt_ref, q_ref, bias_ref, g_ref, later_ref, ones_ref, hm_ref, bd_ref, *rest, heads, pps):
    del pt_ref
    kn, vn = rest[0:1], rest[1:2]
    kp, vp = rest[2:2 + pps], rest[2 + pps:2 + 2 * pps]
    o_ref, carry_ref, acc_ref = rest[2 + 2 * pps:]
    pg = pl.program_id(1)
    qx = q_ref[0]
    rows = qx.shape[0]
    bias = bias_ref[...]
    later = later_ref[...]
    ones = ones_ref[...]

    def sweep(k_refs, v_refs, carry, acc, mask):
        n = len(k_refs)
        z = [_dot(qx, k_refs[i][...].astype(BF16)) + bias for i in range(n)]
        sp = [_softplus(x) for x in z]
        if mask is not None:
            sp = [jnp.where(mask, s, 0.0) for s in sp]
        spb = [s.astype(BF16) for s in sp]
        after = [_dot(s, later) for s in spb]
        total = [_dot(s, ones) for s in spb]
        w = []
        for i in range(n):
            wi = jnp.exp(z[i] - sp[i] - after[i] - carry)
            w.append(wi if mask is None else jnp.where(mask, wi, 0.0))
            carry = carry + total[i]
        pv = [_dot_nt(w[i].astype(BF16), v_refs[i][...].astype(BF16)) for i in range(n)]
        for i in range(n):
            acc = acc + pv[i]
        return carry, acc

    def store_state(carry, acc):
        carry_ref[...] = carry
        acc_ref[...] = acc

    @pl.when(pg == 0)
    def _():
        t = lax.broadcasted_iota(jnp.int32, (rows, PAGE_SIZE), 0) // heads
        s = lax.broadcasted_iota(jnp.int32, (rows, PAGE_SIZE), 1)
        store_state(*sweep(kn, vn, jnp.zeros((rows, PAGE_SIZE), F32), jnp.zeros(acc_ref.shape, F32), s < t))

    store_state(*sweep(kp, vp, carry_ref[...], acc_ref[...], None))

    @pl.when(pg == pl.num_programs(1) - 1)
    def _():
        a = acc_ref[...] * hm_ref[...]
        o = jnp.sum(a.reshape(rows // heads, heads, a.shape[1]), axis=1)
        ms = _dot_exact_rhs(o * o, bd_ref[...]) * (1.0 / HEAD_DIM)
        o_ref[0] = (o * lax.rsqrt(ms + NORM_EPS) * g_ref[...]).astype(BF16)


def sb_sample(q, k_new, v_new, bias, g_sb, cache_k, cache_v, layer, page_table):
    b, t, heads, dh = q.shape
    w = heads * dh
    tp = SAMPLE_T_PAD
    pps = PAGES_PER_STEP
    n_pages = page_table.shape[1]
    rows = tp * heads
    keys_last = lambda x: jnp.moveaxis(x, -3, -1).reshape(x.shape[:-3] + (w, x.shape[-3]))
    new_page = lambda x: jnp.pad(keys_last(x), ((0, 0), (0, 0), (0, PAGE_SIZE - t)))
    head_of_col = jnp.arange(w) // dh
    head_of_row = jnp.arange(rows) % heads
    hmask = head_of_row[:, None] == head_of_col[None, :]
    qp = jnp.pad(q.reshape(b, t, w), ((0, 0), (0, tp - t), (0, 0)))
    qx = jnp.where(hmask[None], jnp.repeat(qp, heads, axis=1), jnp.zeros((), BF16))
    bias_full = jnp.broadcast_to(bias[head_of_row][:, None], (rows, PAGE_SIZE)).astype(F32)
    kern = functools.partial(_sb_sample_kernel, heads=heads, pps=pps)
    c2 = lambda i, p, pt: (0, 0)
    per_b = lambda i, p, pt: (i, 0, 0)
    new_spec = pl.BlockSpec((None, w, PAGE_SIZE), per_b)
    page = lambda s: (lambda i, p, pt: (layer, pt[i, n_pages - 1 - (p * pps + s)], 0, 0))
    page_specs = [pl.BlockSpec((None, None, w, PAGE_SIZE), page(s)) for s in range(pps)]
    grid_spec = pltpu.PrefetchScalarGridSpec(
        num_scalar_prefetch=1,
        grid=(b, n_pages // pps),
        in_specs=[
            pl.BlockSpec((1, rows, w), per_b),
            pl.BlockSpec((rows, PAGE_SIZE), c2),
            pl.BlockSpec((1, w), c2),
            pl.BlockSpec((PAGE_SIZE, PAGE_SIZE), c2),
            pl.BlockSpec((PAGE_SIZE, PAGE_SIZE), c2),
            pl.BlockSpec((rows, w), c2),
            pl.BlockSpec((HEAD_SUM_TILE, HEAD_SUM_TILE), c2),
            new_spec, new_spec,
        ] + page_specs + page_specs,
        out_specs=pl.BlockSpec((1, tp, w), per_b),
        scratch_shapes=[
            pltpu.VMEM((rows, PAGE_SIZE), F32),
            pltpu.VMEM((rows, w), F32),
        ],
    )
    o = pl.pallas_call(
        kern,
        grid_spec=grid_spec,
        out_shape=jax.ShapeDtypeStruct((b, tp, w), BF16),
        compiler_params=_params("parallel", "arbitrary"),
        name="sb_sample",
    )(page_table, qx, bias_full, g_sb, _later_matrix(PAGE_SIZE), jnp.ones((PAGE_SIZE, PAGE_SIZE), BF16),
      hmask.astype(F32), _head_block_ones(HEAD_SUM_TILE), new_page(k_new), new_page(v_new),
      *([keys_last(cache_k)] * pps), *([keys_last(cache_v)] * pps))
    return o[:, :t].reshape(b, t, heads, dh)


def _rwkv_prep_kernel(*refs, rw, halo_mode, tiles_per_seq):
    if halo_mode:
        u_ref, halo_ref, st_ref = refs[:3]
        rest = refs[3:]
    else:
        u_ref, up_ref = refs[:2]
        rest = refs[2:]
    (mu_ref, w0_ref, w2h_ref, w2l_ref, a0_ref, a2h_ref, a2l_ref, g2h_ref, g2l_ref,
     kkw_ref, ka_ref, rk_ref, bd_ref,
     r_ref, k_ref, v_ref, lw_ref, kk_ref, b_ref, gate_ref, bonus_ref) = rest
    u = u_ref[...]
    if halo_mode:
        tm = u.shape[0]
        row = lax.broadcasted_iota(jnp.int32, (tm, 1), 0)
        first = (pl.program_id(0) % tiles_per_seq) == 0
        prev_row = jnp.where(first, st_ref[0], halo_ref[7:8, :])
        up = jnp.where(row == 0, prev_row, pltpu.roll(u, 1, axis=0))
    else:
        up = up_ref[...]
    us = u + (up - u) * mu_ref[...]
    r = us[:, 0:rw]
    k = us[:, rw:2 * rw]
    v = us[:, 2 * rw:3 * rw]
    o = 3 * rw
    w_lo = us[:, o:o + LANES]
    a_lo = us[:, o + LANES:o + 2 * LANES]
    g_lo = us[:, o + 2 * LANES:o + 4 * LANES]
    d = w0_ref[...] + _dot3(jnp.tanh(w_lo), w2h_ref[...], w2l_ref[...])
    w_log = -_softplus(-d) - 0.5
    lw_ref[...] = -jnp.exp(w_log)
    a = _sigmoid(a0_ref[...] + _dot3(a_lo, a2h_ref[...], a2l_ref[...]))
    gate_ref[...] = _dot3(_sigmoid(g_lo), g2h_ref[...], g2l_ref[...])
    kkr = k * kkw_ref[...]
    nrm = jnp.sqrt(_dot_exact_rhs(kkr * kkr, bd_ref[...]))
    kk = kkr / jnp.maximum(nrm, KK_EPS)
    k_eff = k * (1.0 + (a - 1.0) * ka_ref[...])
    r_ref[...] = r
    k_ref[...] = k_eff
    v_ref[...] = v
    kk_ref[...] = kk
    b_ref[...] = kk * a
    bonus_ref[...] = _dot_exact_rhs(r * k_eff * rk_ref[...], bd_ref[...]) * v


def rwkv_prep(u, prev, pw, rw, seq_len, *, tm, halo_mode):
    n, uw = u.shape
    tm = min(tm, n)
    c = lambda i: (0, 0)
    row_spec = pl.BlockSpec((tm, uw), lambda i: (i, 0))
    if halo_mode:
        tps = seq_len // tm
        halo_blk = tm // 8
        lead_specs = [row_spec,
                      pl.BlockSpec((8, uw), lambda i: (jnp.maximum(i * halo_blk - 1, 0), 0)),
                      pl.BlockSpec((1, 1, uw), lambda i: (i // tps, 0, 0))]
        lead = (u, u, prev)
    else:
        tps = 1
        lead_specs = [row_spec, row_spec]
        lead = (u, prev)
    vec = lambda width: pl.BlockSpec((1, width), c)
    mat = lambda rows: pl.BlockSpec((rows, rw), c)
    in_specs = lead_specs + [
        vec(uw), vec(rw), mat(LANES), mat(LANES), vec(rw), mat(LANES), mat(LANES),
        mat(2 * LANES), mat(2 * LANES), vec(rw), vec(rw), vec(rw), pl.BlockSpec(pw['bd'].shape, c)]
    out_spec = pl.BlockSpec((tm, rw), lambda i: (i, 0))
    kern = functools.partial(_rwkv_prep_kernel, rw=rw, halo_mode=halo_mode, tiles_per_seq=tps)
    return pl.pallas_call(
        kern,
        grid=(n // tm,),
        in_specs=in_specs,
        out_specs=[out_spec] * 8,
        out_shape=[jax.ShapeDtypeStruct((n, rw), F32)] * 8,
        compiler_params=_params("parallel"),
        name="rwkv_prep",
    )(*lead, pw['mu'], pw['w0'], pw['w2h'], pw['w2l'], pw['a0'], pw['a2h'], pw['a2l'],
      pw['g2h'], pw['g2l'], pw['kk'], pw['ka'], pw['rk'], pw['bd'])


def _rwkv_chunk_kernel(r_ref, k_ref, v_ref, lw_ref, kk_ref, b_ref, y_ref, s_out_ref, s_ref, *, chunk, npairs):
    c = pl.program_id(1)
    C = chunk
    C2 = HEADS_PER_TILE * C

    @pl.when(c == 0)
    def _():
        s_ref[...] = jnp.zeros_like(s_ref)

    ci = lax.broadcasted_iota(jnp.int32, (C, C), 0)
    cj = lax.broadcasted_iota(jnp.int32, (C, C), 1)
    ti = lax.broadcasted_iota(jnp.int32, (C2, C2), 0)
    tj = lax.broadcasted_iota(jnp.int32, (C2, C2), 1)
    same_blk = (ti // C) == (tj // C)
    incl = same_blk & (ti >= tj)
    strict = same_blk & (ti > tj)
    eye = (ti == tj).astype(F32)
    lane_head = lax.broadcasted_iota(jnp.int32, (C2, LANES), 1) // HEAD_DIM
    row_head = lax.broadcasted_iota(jnp.int32, (C2, LANES), 0) // C
    own_lanes = lane_head == row_head

    def stack(x):
        return jnp.where(own_lanes, jnp.concatenate([x] * HEADS_PER_TILE, axis=0), jnp.zeros((), BF16))

    lw = lw_ref[...]
    l_hi, l_mid, l_lo = _split3(lw)
    tri = (ci >= cj).astype(BF16)
    cl = _dot(tri, l_hi) + (_dot(tri, l_mid) + _dot(tri, l_lo))
    cl_end = cl[C - 1:C, :]
    e_pos = jnp.exp(cl)
    e_neg = jnp.exp(-cl)
    e_prev = jnp.exp(cl - lw)
    e_end = jnp.exp(cl_end - cl)
    all_ones = jnp.ones((C, LANES), BF16)
    d_rows = jnp.exp(_dot_tn(l_hi, all_ones) + (_dot_tn(l_mid, all_ones) + _dot_tn(l_lo, all_ones)))
    r = r_ref[...]
    k = k_ref[...]
    b = b_ref[...]
    rt = (r * e_pos).astype(BF16)
    kt = (kk_ref[...] * e_prev).astype(BF16)
    k_inv = (k * e_neg).astype(BF16)
    b_inv = (b * e_neg).astype(BF16)
    k_dec = (k * e_end).astype(BF16)
    b_dec = (b * e_end).astype(BF16)
    vb = v_ref[...].astype(BF16)

    tiles = range(npairs)
    sls = [slice(p * LANES, (p + 1) * LANES) for p in tiles]
    st = [s_ref[p] for p in tiles]
    stb = [s.astype(BF16) for s in st]
    v2 = [stack(vb[:, sl]) for sl in sls]
    kt2 = [stack(kt[:, sl]) for sl in sls]
    rt2 = [stack(rt[:, sl]) for sl in sls]
    ab = [_dot_nt(jnp.concatenate([kt2[p], rt2[p]], axis=0),
                  jnp.concatenate([stack(k_inv[:, sls[p]]), stack(b_inv[:, sls[p]])], axis=0))
          for p in tiles]
    a_m = [jnp.where(strict, m[:C2, :C2], 0.0).astype(BF16) for m in ab]
    n_m = [jnp.where(strict, -m[:C2, C2:], 0.0) for m in ab]
    aq_m = [jnp.where(incl, m[C2:, :C2], 0.0).astype(BF16) for m in ab]
    nbq_m = [jnp.where(incl, -m[C2:, C2:], 0.0).astype(BF16) for m in ab]
    g = [eye + m for m in n_m]
    pw = [_dot(m.astype(BF16), m.astype(BF16)) for m in n_m]
    levels = int(math.log2(C))
    for lvl in range(1, levels):
        if lvl < levels - 1:
            gp = [_dot(jnp.concatenate([g[p], pw[p]], axis=0).astype(BF16), pw[p].astype(BF16)) for p in tiles]
            g = [g[p] + gp[p][:C2] for p in tiles]
            pw = [m[C2:] for m in gp]
        else:
            g = [g[p] + _dot(g[p].astype(BF16), pw[p].astype(BF16)) for p in tiles]
    x = [_dot(jnp.concatenate([a_m[p], kt2[p]], axis=1), jnp.concatenate([v2[p], stb[p]], axis=0)) for p in tiles]
    u2b = [_dot(g[p].astype(BF16), x[p].astype(BF16)).astype(BF16) for p in tiles]
    y2 = [_dot(jnp.concatenate([aq_m[p], nbq_m[p], rt2[p]], axis=1),
               jnp.concatenate([v2[p], u2b[p], stb[p]], axis=0)) for p in tiles]
    upd = [_dot_tn(jnp.concatenate([stack(k_dec[:, sls[p]]), stack(b_dec[:, sls[p]])], axis=0),
                   jnp.concatenate([v2[p], -u2b[p]], axis=0)) for p in tiles]
    for p in tiles:
        y = y2[p][:C]
        for hh in range(1, HEADS_PER_TILE):
            y = y + y2[p][hh * C:(hh + 1) * C]
        y_ref[:, sls[p]] = y
        s_new = st[p] * d_rows[sls[p], :] + upd[p]
        s_ref[p] = s_new
        s_out_ref[0, p] = s_new


def rwkv_chunk(r, k, v, lw, kk, b, batch, seq_len):
    n, rw = r.shape
    npairs = rw // LANES
    C = RW_CHUNK
    nc = seq_len // C
    blk = pl.BlockSpec((C, rw), lambda i, c: (i * nc + c, 0))
    kern = functools.partial(_rwkv_chunk_kernel, chunk=C, npairs=npairs)
    return pl.pallas_call(
        kern,
        grid=(batch, nc),
        in_specs=[blk] * 6,
        out_specs=[blk, pl.BlockSpec((1, npairs, LANES, LANES), lambda i, c: (i, 0, 0, 0))],
        out_shape=[jax.ShapeDtypeStruct((n, rw), F32),
                   jax.ShapeDtypeStruct((batch, npairs, LANES, LANES), F32)],
        scratch_shapes=[pltpu.VMEM((npairs, LANES, LANES), F32)],
        compiler_params=_params("parallel", "arbitrary"),
        name="rwkv_chunk",
    )(r, k, v, lw, kk, b)


def _rwkv_seq_kernel(s0_ref, r_ref, k_ref, lw_ref, kk_ref, b_ref, v_ref, y_ref, s_out_ref, *, steps):
    s = s0_ref[0]
    dh = s.shape[-1]
    eye = (lax.broadcasted_iota(jnp.int32, (dh, dh), 0) == lax.broadcasted_iota(jnp.int32, (dh, dh), 1)).astype(F32)
    for t in range(steps):
        vcol = jnp.sum(v_ref[0, t] * eye, axis=-1, keepdims=True)
        sa = jnp.sum(s * kk_ref[0, t], axis=-1, keepdims=True)
        s = s * jnp.exp(lw_ref[0, t]) - sa * b_ref[0, t] + vcol * k_ref[0, t]
        ycol = jnp.sum(s * r_ref[0, t], axis=-1, keepdims=True)
        y_ref[0, t] = jnp.sum(ycol * eye, axis=-2, keepdims=True)
    s_out_ref[0] = s


def rwkv_seq(s0, r, k, v, lw, kk, b):
    bsz, heads, dh, _ = s0.shape
    t = r.shape[1]
    rowv = lambda x: x.reshape(bsz, t, heads, 1, dh)
    row_spec = pl.BlockSpec((1, t, heads, 1, dh), lambda i: (i, 0, 0, 0, 0))
    st_spec = pl.BlockSpec((1, heads, dh, dh), lambda i: (i, 0, 0, 0))
    kern = functools.partial(_rwkv_seq_kernel, steps=t)
    y, s = pl.pallas_call(
        kern,
        grid=(bsz,),
        in_specs=[st_spec] + [row_spec] * 6,
        out_specs=[row_spec, st_spec],
        out_shape=[jax.ShapeDtypeStruct((bsz, t, heads, 1, dh), F32),
                   jax.ShapeDtypeStruct(s0.shape, F32)],
        compiler_params=_params("parallel"),
        name="rwkv_seq",
    )(s0, rowv(r), rowv(k), rowv(lw), rowv(kk), rowv(b), rowv(v))
    return y.reshape(bsz, t, heads * dh), s


def _rwkv_epi_kernel(y_ref, bonus_ref, gate_ref, g_ref, b_ref, bd_ref, o_ref):
    y = y_ref[...]
    inv = 1.0 / HEAD_DIM
    mu = _dot_exact_rhs(y, bd_ref[...]) * inv
    yc = y - mu
    var = _dot_exact_rhs(yc * yc, bd_ref[...]) * inv
    yn = yc * lax.rsqrt(var + LNX_EPS) * g_ref[...] + b_ref[...]
    o_ref[...] = ((yn + bonus_ref[...]) * gate_ref[...]).astype(BF16)


def rwkv_epilogue(y, bonus, gate, ln_g, ln_b, bd, *, tm):
    n, rw = y.shape
    tm = min(tm, n)
    blk = pl.BlockSpec((tm, rw), lambda i: (i, 0))
    vec = pl.BlockSpec((1, rw), lambda i: (0, 0))
    return pl.pallas_call(
        _rwkv_epi_kernel,
        grid=(n // tm,),
        in_specs=[blk, blk, blk, vec, vec, pl.BlockSpec(bd.shape, lambda i: (0, 0))],
        out_specs=blk,
        out_shape=jax.ShapeDtypeStruct((n, rw), BF16),
        compiler_params=_params("parallel"),
        name="rwkv_epilogue",
    )(y, bonus, gate, ln_g, ln_b, bd)


def _pad_rows(m, rows):
    return jnp.pad(m, ((0, rows - m.shape[0]), (0, 0)))


def _hi_lo(m):
    hi = m.astype(BF16)
    return hi, (m - hi.astype(F32)).astype(BF16)


def _u_layout(rw, n_decay, n_icl, n_gate):
    segs = [(3 * rw, 3 * rw), (n_decay, LANES), (n_icl, LANES), (n_gate, 2 * LANES)]
    assert n_decay <= LANES and n_icl <= LANES and n_gate <= 2 * LANES
    return segs


def _pad_u_cols(m, segs):
    out, o = [], 0
    for width, padded in segs:
        out.append(m[..., o:o + width])
        if padded > width:
            out.append(jnp.zeros(m.shape[:-1] + (padded - width,), m.dtype))
        o += width
    return jnp.concatenate(out, axis=-1)


def _unpad_u_cols(m, segs):
    out, o = [], 0
    for width, padded in segs:
        out.append(m[..., o:o + width])
        o += padded
    return jnp.concatenate(out, axis=-1)


def _prepare_weights(w_in, w_out, g_pre_mix, g_post_mix, g_pre_ffn, g_post_ffn, g_sb_out, sb_bias,
                     rw_mu, rw_w0, rw_w2, rw_a0, rw_a2, rw_g2, rw_kk, rw_ka, rw_rk, rw_ln_g, rw_ln_b,
                     w_up, conv_w, conv_b, w_down):
    rw = rw_w0.shape[0]
    sbw = g_sb_out.shape[0]
    segs = _u_layout(rw, rw_w2.shape[0], rw_a2.shape[0], rw_g2.shape[0])
    row = lambda x: x.reshape(1, -1)
    w2h, w2l = _hi_lo(_pad_rows(rw_w2, LANES))
    a2h, a2l = _hi_lo(_pad_rows(rw_a2, LANES))
    g2h, g2l = _hi_lo(_pad_rows(rw_g2, 2 * LANES))
    bd = _head_block_ones(HEAD_SUM_TILE)
    w_in_p = jnp.concatenate([w_in[:, :3 * sbw], _pad_u_cols(w_in[:, 3 * sbw:], segs)], axis=1).astype(BF16)
    prep = dict(mu=row(_pad_u_cols(rw_mu, segs)), w0=row(rw_w0), w2h=w2h, w2l=w2l, a0=row(rw_a0),
                a2h=a2h, a2l=a2l, g2h=g2h, g2l=g2l, kk=row(rw_kk), ka=row(rw_ka),
                rk=row(rw_rk), bd=bd)
    return dict(
        segs=segs, rw=rw, sbw=sbw, w_in=w_in_p, w_kv_t=w_in[:, sbw:3 * sbw].T.astype(BF16),
        w_out=w_out.astype(BF16),
        g_pre_mix=row(g_pre_mix), g_post_mix=row(g_post_mix), g_pre_ffn=row(g_pre_ffn),
        g_post_ffn=row(g_post_ffn), g_sb=row(g_sb_out), sb_bias=sb_bias, prep=prep,
        ln_g=row(rw_ln_g), ln_b=row(rw_ln_b), bd=bd,
        w_up=w_up.astype(BF16), conv_w=conv_w, conv_b=row(conv_b), w_down=w_down.astype(BF16))


def _tail(wts, x, o_sb, o_rw, *, tm):
    mixed_in = jnp.concatenate([o_sb, o_rw], axis=-1)
    return mm_norm_resid(mixed_in, wts['w_out'], x, wts['g_post_mix'], wts['g_pre_ffn'],
                         tm=tm, tk=mixed_in.shape[1], emit_h=True)


def _down(wts, act, x1, *, tm):
    (y,) = mm_norm_resid(act, wts['w_down'], x1, wts['g_post_ffn'], wts['g_post_ffn'],
                         tm=tm, tk=act.shape[1] // 4, emit_h=False)
    return y


def _prompt_layer(wts, x3):
    bsz, t, d = x3.shape
    n = bsz * t
    x = x3.reshape(n, d)
    rw, sbw, segs = wts['rw'], wts['sbw'], wts['segs']
    uw = sum(p for _, p in segs)
    q, k_t, v_t, u = in_proj(x, wts['g_pre_mix'], wts['w_in'], sbw, uw, tm=min(1024, t),
                             w_kv_t=wts['w_kv_t'], seq_len=t)
    o_sb = sb_prompt(q, k_t, v_t, wts['sb_bias'], wts['g_sb'], bsz, t, tq=min(1024, t), tk=min(256, t))
    shift0 = jnp.zeros((bsz, 1, uw), F32)
    r, ke, vr, lw, kk, b, gate, bonus = rwkv_prep(u, shift0, wts['prep'], rw, t, tm=min(256, t), halo_mode=True)
    y, s_pairs = rwkv_chunk(r, ke, vr, lw, kk, b, bsz, t)
    o_rw = rwkv_epilogue(y, bonus, gate, wts['ln_g'], wts['ln_b'], wts['bd'], tm=512)
    x1, h2 = _tail(wts, x, o_sb, o_rw, tm=512)
    tm_f = min(1024, t)
    conv0 = jnp.zeros((bsz, CONV_W - 1, wts['conv_w'].shape[1]), F32)
    act, gtail = ffn_up_prompt(h2, wts['w_up'], wts['conv_w'], wts['conv_b'], conv0, t, tm=tm_f)
    yout = _down(wts, act, x1, tm=512)
    heads = rw // HEAD_DIM
    hp = HEADS_PER_TILE
    sp = s_pairs.reshape(bsz, rw // LANES, hp, HEAD_DIM, hp, HEAD_DIM)
    wkv = jnp.stack([jnp.swapaxes(sp[:, :, i, :, i, :], -1, -2) for i in range(hp)], axis=2)
    wkv = wkv.reshape(bsz, heads, HEAD_DIM, HEAD_DIM)
    shift = _unpad_u_cols(u.reshape(bsz, t, uw)[:, -1], segs)
    tps = t // tm_f
    conv_new = gtail.reshape(bsz, tps, 8, -1)[:, -1, 8 - (CONV_W - 1):]
    keys_first = lambda a: jnp.moveaxis(a.reshape(bsz, sbw // HEAD_DIM, HEAD_DIM, t), -1, 1)
    return yout.reshape(bsz, t, d), keys_first(k_t), keys_first(v_t), wkv, shift, conv_new


def _sample_layer(wts, x3, cache_k, cache_v, layer, page_table, wkv0, shift_prev, conv_prev):
    bsz, t, d = x3.shape
    n = bsz * t
    x = x3.reshape(n, d)
    rw, sbw, segs = wts['rw'], wts['sbw'], wts['segs']
    uw = sum(p for _, p in segs)
    q, k, v, u = in_proj(x, wts['g_pre_mix'], wts['w_in'], sbw, uw, tm=n)
    kvshape = (bsz, t, sbw // HEAD_DIM, HEAD_DIM)
    o_sb = sb_sample(q.reshape(kvshape), k.reshape(kvshape), v.reshape(kvshape),
                     wts['sb_bias'], wts['g_sb'], cache_k, cache_v, layer, page_table).reshape(n, sbw)
    u3 = u.reshape(bsz, t, uw)
    u_prev = jnp.concatenate([_pad_u_cols(shift_prev, segs)[:, None], u3[:, :-1]], axis=1).reshape(n, uw)
    r, ke, vr, lw, kk, b, gate, bonus = rwkv_prep(u, u_prev, wts['prep'], rw, t, tm=n, halo_mode=False)
    as3 = lambda a: a.reshape(bsz, t, rw)
    y3, wkv = rwkv_seq(wkv0, as3(r), as3(ke), as3(vr), as3(lw), as3(kk), as3(b))
    o_rw = rwkv_epilogue(y3.reshape(n, rw), bonus, gate, wts['ln_g'], wts['ln_b'], wts['bd'], tm=n)
    x1, h2 = _tail(wts, x, o_sb, o_rw, tm=n)
    f = wts['conv_w'].shape[1]
    zeros = jnp.zeros((bsz, t, f), F32)
    p1 = zeros.at[:, 0].set(conv_prev[:, 1]).reshape(n, f)
    p2 = zeros.at[:, 0].set(conv_prev[:, 0]).at[:, 1].set(conv_prev[:, 1]).reshape(n, f)
    act, gfull = ffn_up_sample(h2, wts['w_up'], wts['conv_w'], wts['conv_b'], p1, p2, t)
    yout = _down(wts, act, x1, tm=n)
    shift = _unpad_u_cols(u3[:, -1], segs)
    conv_new = gfull.reshape(bsz, t, f)[:, t - (CONV_W - 1):]
    kvshape = (bsz, t, sbw // HEAD_DIM, HEAD_DIM)
    return yout.reshape(bsz, t, d), k.reshape(kvshape), v.reshape(kvshape), wkv, shift, conv_new


def kernel(x_prompt, x_sample, cache_sb_k, cache_sb_v, state_rwkv_wkv, state_rwkv_shift, state_ffn_conv, page_table, w_in, w_out, g_pre_mix, g_post_mix, g_pre_ffn, g_post_ffn, g_sb_out, sb_bias, rw_mu, rw_w0, rw_w2, rw_a0, rw_a2, rw_g2, rw_kk, rw_ka, rw_rk, rw_ln_g, rw_ln_b, w_up, conv_w, conv_b, w_down):
    depth = w_in.shape[0]
    yp, ys = x_prompt, x_sample
    outs_p, outs_s = [], []
    for l in range(depth):
        wts = _prepare_weights(w_in[l], w_out[l], g_pre_mix[l], g_post_mix[l], g_pre_ffn[l], g_post_ffn[l],
                               g_sb_out[l], sb_bias[l], rw_mu[l], rw_w0[l], rw_w2[l], rw_a0[l], rw_a2[l],
                               rw_g2[l], rw_kk[l], rw_ka[l], rw_rk[l].reshape(-1), rw_ln_g[l], rw_ln_b[l],
                               w_up[l], conv_w[l], conv_b[l], w_down[l])
        yp, *rest_p = _prompt_layer(wts, yp)
        ys, *rest_s = _sample_layer(wts, ys, cache_sb_k, cache_sb_v, l, page_table,
                                    state_rwkv_wkv[l], state_rwkv_shift[l], state_ffn_conv[l])
        outs_p.append(rest_p)
        outs_s.append(rest_s)
    stack = lambda outs, i: jnp.stack([o[i] for o in outs])
    return (yp, ys,
            stack(outs_p, 0), stack(outs_p, 1), stack(outs_p, 2), stack(outs_p, 3), stack(outs_p, 4),
            stack(outs_s, 0), stack(outs_s, 1), stack(outs_s, 2), stack(outs_s, 3), stack(outs_s, 4))
```
